```python
import math
import jax, jax.numpy as jnp
from jax import lax
import numpy as np

D_MODEL = 2048
BATCH = 4
SEQ = 2048
DEPTH = 4

FOX_HEAD_DIM = 128
FOX_HEADS = D_MODEL // FOX_HEAD_DIM
FOX_WIDTH = FOX_HEADS * FOX_HEAD_DIM
Q_BLOCK = 128
FOX_FORGET_BIAS = 3.0
GDN_HEAD_DIM = 128
GDN_K_HEADS = D_MODEL // GDN_HEAD_DIM
GDN_V_HEADS = 2 * GDN_K_HEADS
GDN_K_WIDTH = GDN_K_HEADS * GDN_HEAD_DIM
GDN_V_WIDTH = GDN_V_HEADS * GDN_HEAD_DIM
GDN_CONV_CH = 2 * GDN_K_WIDTH + GDN_V_WIDTH
GDN_CONV = 4
GDN_CHUNK = 64
FFN_DIM = 5632
N_EXPERTS = 8
TOP_K = 2
EXPERT_DIM = FFN_DIM // 2
N_FOX_LAYERS = (DEPTH + 1) // 2
N_GDN_LAYERS = DEPTH // 2
EPS = 1e-6
F32 = jnp.float32

kernel_name = 'hybrid_fox_gdn_moe_trunk'


def rmsnorm(x, g):
    xf = x.astype(F32)
    y = xf * lax.rsqrt(jnp.mean(xf * xf, axis=-1, keepdims=True) + EPS)
    return (y * g.astype(F32)).astype(x.dtype)


def l2norm(x):
    xf = x.astype(F32)
    return xf * lax.rsqrt(jnp.sum(xf * xf, axis=-1, keepdims=True) + EPS)


def swiglu(t, w_gate, w_up, w_down):
    return (jax.nn.silu(t @ w_gate) * (t @ w_up)) @ w_down


def fox_attention(h, w_in, b_f, q_gain, k_gain, w_out):
    B, S, _ = h.shape
    proj = h @ w_in
    q, k, v, og, f_logit = jnp.split(
        proj, [FOX_WIDTH, 2 * FOX_WIDTH, 3 * FOX_WIDTH, 4 * FOX_WIDTH], axis=-1)
    q = rmsnorm(q.reshape(B, S, FOX_HEADS, FOX_HEAD_DIM), q_gain).transpose(0, 2, 1, 3)
    k = rmsnorm(k.reshape(B, S, FOX_HEADS, FOX_HEAD_DIM), k_gain).transpose(0, 2, 1, 3)
    v = v.reshape(B, S, FOX_HEADS, FOX_HEAD_DIM).transpose(0, 2, 1, 3)
    log_f = jax.nn.log_sigmoid((f_logit + b_f).astype(F32))
    cum = jnp.cumsum(log_f, axis=1).transpose(0, 2, 1)
    scale = FOX_HEAD_DIM ** -0.5
    outs = []
    for blk in range(S // Q_BLOCK):
        lo, hi = blk * Q_BLOCK, (blk + 1) * Q_BLOCK
        s = jnp.einsum('bhqd,bhkd->bhqk', q[:, :, lo:hi], k[:, :, :hi]).astype(F32) * scale
        s = s + cum[:, :, lo:hi, None] - cum[:, :, None, :hi]
        causal = (lo + jnp.arange(Q_BLOCK))[:, None] >= jnp.arange(hi)[None, :]
        p = jax.nn.softmax(jnp.where(causal, s, -jnp.inf), axis=-1)
        outs.append(jnp.einsum('bhqk,bhkd->bhqd', p.astype(v.dtype), v[:, :, :hi]))
    o = jnp.concatenate(outs, axis=2).transpose(0, 2, 1, 3).reshape(B, S, FOX_WIDTH)
    return (o * jax.nn.sigmoid(og)) @ w_out


def causal_conv(x, w):
    K, C = w.shape
    return lax.conv_general_dilated(
        x, w[:, None, :].astype(x.dtype), window_strides=(1,), padding=[(K - 1, 0)],
        dimension_numbers=('NWC', 'WIO', 'NWC'), feature_group_count=C)


def gated_delta_rule(q, k, v, g, beta):
    B, S, H, DK = q.shape
    DV = v.shape[-1]
    C = GDN_CHUNK
    N = S // C

    def chunks(t):
        return jnp.moveaxis(t, 2, 1).reshape((B, H, N, C) + t.shape[3:])

    q = chunks(q) * DK ** -0.5
    k = chunks(k)
    v = chunks(v)
    beta = chunks(beta)
    g = jnp.cumsum(chunks(g), axis=-1)
    incl = jnp.tril(jnp.ones((C, C), dtype=bool))
    strict = jnp.tril(jnp.ones((C, C), dtype=bool), -1)
    decay = jnp.exp(jnp.where(incl, g[..., :, None] - g[..., None, :], -jnp.inf))
    kb = k * beta[..., None]
    vb = v * beta[..., None]
    lmat = jnp.where(strict, jnp.einsum('bhnid,bhnjd->bhnij', kb, k) * decay, 0.0)
    eye = jnp.eye(C, dtype=F32)
    rhs = jnp.concatenate([vb, kb * jnp.exp(g)[..., None]], axis=-1)
    sol = lax.linalg.triangular_solve(eye + lmat, rhs, left_side=True, lower=True,
                                      unit_diagonal=True)
    u, w = sol[..., :DV], sol[..., DV:]
    intra = jnp.where(incl, jnp.einsum('bhnid,bhnjd->bhnij', q, k) * decay, 0.0)

    def step(state, xs):
        q_c, k_c, u_c, w_c, a_c, g_c = xs
        v_new = u_c - jnp.einsum('bhcd,bhde->bhce', w_c, state)
        o = (jnp.einsum('bhcd,bhde->bhce', q_c * jnp.exp(g_c)[..., None], state)
             + jnp.einsum('bhij,bhje->bhie', a_c, v_new))
        g_last = g_c[..., -1]
        k_dec = k_c * jnp.exp(g_last[..., None] - g_c)[..., None]
        state = (state * jnp.exp(g_last)[..., None, None]
                 + jnp.einsum('bhcd,bhce->bhde', k_dec, v_new))
        return state, o

    xs = tuple(jnp.moveaxis(t, 2, 0) for t in (q, k, u, w, intra, g))
    state0 = jnp.zeros((B, H, DK, DV), F32)
    _, o = lax.scan(step, state0, xs)
    return jnp.moveaxis(o, 0, 2).reshape(B, H, S, DV).transpose(0, 2, 1, 3)


def gdn_mixer(h, w_in, conv_w, a_log, dt_bias, o_gain, w_out):
    B, S, _ = h.shape
    proj = h @ w_in
    qkv, z, b, a = jnp.split(
        proj, [GDN_CONV_CH, GDN_CONV_CH + GDN_V_WIDTH, GDN_CONV_CH + GDN_V_WIDTH + GDN_V_HEADS],
        axis=-1)
    qkv = jax.nn.silu(causal_conv(qkv, conv_w))
    q, k, v = jnp.split(qkv, [GDN_K_WIDTH, 2 * GDN_K_WIDTH], axis=-1)
    rep = GDN_V_HEADS // GDN_K_HEADS
    q = jnp.repeat(l2norm(q.reshape(B, S, GDN_K_HEADS, GDN_HEAD_DIM)), rep, axis=2)
    k = jnp.repeat(l2norm(k.reshape(B, S, GDN_K_HEADS, GDN_HEAD_DIM)), rep, axis=2)
    v = v.reshape(B, S, GDN_V_HEADS, GDN_HEAD_DIM).astype(F32)
    beta = jax.nn.sigmoid(b.astype(F32))
    g = -jnp.exp(a_log.astype(F32)) * jax.nn.softplus(a.astype(F32) + dt_bias.astype(F32))
    o = gated_delta_rule(q, k, v, g, beta).astype(h.dtype)
    o = rmsnorm(o, o_gain) * jax.nn.silu(z.reshape(B, S, GDN_V_HEADS, GDN_HEAD_DIM))
    return o.reshape(B, S, GDN_V_WIDTH) @ w_out


def moe_ffn(h, router, w_gate, w_up, w_down):
    B, S, D = h.shape
    t = h.reshape(B * S, D)
    logits = (t @ router).astype(F32)
    top_v, top_i = lax.top_k(logits, TOP_K)
    wts = jax.nn.softmax(top_v, axis=-1)
    gates = jnp.sum(jax.nn.one_hot(top_i, N_EXPERTS, dtype=F32) * wts[..., None], axis=1)
    out = jnp.zeros_like(t)
    for e in range(N_EXPERTS):
        out = out + gates[:, e:e + 1].astype(t.dtype) * swiglu(t, w_gate[e], w_up[e], w_down[e])
    return out.reshape(B, S, D)


def setup_inputs(seed: int = 0) -> dict:
    key = jax.random.key(seed)
    ks = jax.random.split(key, 21)
    NA, NB = N_FOX_LAYERS, N_GDN_LAYERS
    fox_in = 4 * FOX_WIDTH + FOX_HEADS
    gdn_in = GDN_CONV_CH + GDN_V_WIDTH + 2 * GDN_V_HEADS
    out_gain = 0.5

    def nrm(k, shape, fan_in, gain=1.0):
        return gain * fan_in ** -0.5 * jax.random.normal(k, shape, F32)

    def gain(k, shape):
        return 1.0 + 0.02 * jax.random.normal(k, shape, F32)

    dt = jnp.exp(jax.random.uniform(ks[11], (NB, GDN_V_HEADS), F32,
                                    math.log(1e-3), math.log(1e-1)))
    return {
        'x': jax.random.normal(ks[0], (BATCH, SEQ, D_MODEL), F32),
        'norm_mix': gain(ks[1], (DEPTH, D_MODEL)),
        'norm_ffn': gain(ks[2], (DEPTH, D_MODEL)),
        'fox_w_in': nrm(ks[3], (NA, D_MODEL, fox_in), D_MODEL),
        'fox_b_f': FOX_FORGET_BIAS + 0.5 * jax.random.normal(ks[4], (NA, FOX_HEADS), F32),
        'fox_q_norm': gain(ks[5], (NA, FOX_HEAD_DIM)),
        'fox_k_norm': gain(ks[6], (NA, FOX_HEAD_DIM)),
        'fox_w_out': nrm(ks[7], (NA, FOX_WIDTH, D_MODEL), FOX_WIDTH, out_gain),
        'gdn_w_in': nrm(ks[8], (NB, D_MODEL, gdn_in), D_MODEL),
        'gdn_conv': nrm(ks[9], (NB, GDN_CONV, GDN_CONV_CH), GDN_CONV),
        'gdn_a_log': jnp.log(jax.random.uniform(ks[10], (NB, GDN_V_HEADS), F32, 1.0, 16.0)),
        'gdn_dt_bias': jnp.log(jnp.expm1(dt)),
        'gdn_o_norm': gain(ks[12], (NB, GDN_HEAD_DIM)),
        'gdn_w_out': nrm(ks[13], (NB, GDN_V_WIDTH, D_MODEL), GDN_V_WIDTH, out_gain),
        'ffn_w_gate': nrm(ks[14], (NA, D_MODEL, FFN_DIM), D_MODEL),
        'ffn_w_up': nrm(ks[15], (NA, D_MODEL, FFN_DIM), D_MODEL),
        'ffn_w_down': nrm(ks[16], (NA, FFN_DIM, D_MODEL), FFN_DIM, out_gain),
        'moe_router': nrm(ks[17], (NB, D_MODEL, N_EXPERTS), D_MODEL),
        'moe_w_gate': nrm(ks[18], (NB, N_EXPERTS, D_MODEL, EXPERT_DIM), D_MODEL),
        'moe_w_up': nrm(ks[19], (NB, N_EXPERTS, D_MODEL, EXPERT_DIM), D_MODEL),
        'moe_w_down': nrm(ks[20], (NB, N_EXPERTS, EXPERT_DIM, D_MODEL), EXPERT_DIM, out_gain),
    }


def reference(x, norm_mix, norm_ffn, fox_w_in, fox_b_f, fox_q_norm, fox_k_norm, fox_w_out,
              gdn_w_in, gdn_conv, gdn_a_log, gdn_dt_bias, gdn_o_norm, gdn_w_out,
              ffn_w_gate, ffn_w_up, ffn_w_down,
              moe_router, moe_w_gate, moe_w_up, moe_w_down):
    h = x
    for i in range(DEPTH):
        j = i // 2
        hn = rmsnorm(h, norm_mix[i])
        if i % 2 == 0:
            h = h + fox_attention(hn, fox_w_in[j], fox_b_f[j], fox_q_norm[j], fox_k_norm[j],
                                  fox_w_out[j])
            h = h + swiglu(rmsnorm(h, norm_ffn[i]), ffn_w_gate[j], ffn_w_up[j], ffn_w_down[j])
        else:
            h = h + gdn_mixer(hn, gdn_w_in[j], gdn_conv[j], gdn_a_log[j], gdn_dt_bias[j],
                              gdn_o_norm[j], gdn_w_out[j])
            h = h + moe_ffn(rmsnorm(h, norm_ffn[i]), moe_router[j], moe_w_gate[j],
                            moe_w_up[j], moe_w_down[j])
    return h
```

```python
import functools
import math

import jax
import jax.numpy as jnp
from jax import lax
from jax.experimental import pallas as pl
from jax.experimental.pallas import tpu as pltpu

F32 = jnp.float32
BF16 = jnp.bfloat16
EPS = 1e-6
LANES = 128
HEAD_DIM = 128
GDN_CHUNK = 64
GDN_CONV = 4
N_EXPERTS = 8
VMEM_LIMIT_BYTES = 56 * 1024 * 1024
HIGHEST = lax.Precision.HIGHEST


def _params(*sem):
    return pltpu.CompilerParams(dimension_semantics=sem, vmem_limit_bytes=VMEM_LIMIT_BYTES)


def _log1pexp_neg_abs(x):
    return jnp.log(1.0 + jnp.exp(-jnp.abs(x)))


def _log_sigmoid(x):
    return jnp.minimum(x, 0.0) - _log1pexp_neg_abs(x)


def _softplus(x):
    return jnp.maximum(x, 0.0) + _log1pexp_neg_abs(x)


def _sigmoid(x):
    return 1.0 / (1.0 + jnp.exp(-x))


def _silu(x):
    return x * _sigmoid(x)


def _dot(a, b):
    return jnp.dot(a, b, preferred_element_type=F32)


def _dot_nt(a, b):
    return lax.dot_general(a, b, (((1,), (1,)), ((), ())), preferred_element_type=F32)


def _dot_tn(a, b):
    return lax.dot_general(a, b, (((0,), (0,)), ((), ())), preferred_element_type=F32)


def _rms(x, g):
    ms = jnp.mean(x * x, axis=-1, keepdims=True)
    return x * lax.rsqrt(ms + EPS) * g


def _norm_kernel(x_ref, g_ref, hn_ref):
    hn_ref[...] = _rms(x_ref[...], g_ref[...]).astype(hn_ref.dtype)


def _norm_small_kernel(x_ref, g_ref, ws_ref, aux_ref, hn_ref, small_ref, *, post):
    y = _rms(x_ref[...], g_ref[...])
    hn_ref[...] = y.astype(hn_ref.dtype)
    s = jnp.dot(y, ws_ref[...], preferred_element_type=F32, precision=HIGHEST)
    small_ref[...] = post(s, aux_ref[...])


def rmsnorm(x, g, *, tm=512):
    T, D = x.shape
    return pl.pallas_call(
        _norm_kernel,
        grid=(T // tm,),
        in_specs=[pl.BlockSpec((tm, D), lambda i: (i, 0)),
                  pl.BlockSpec((1, D), lambda i: (0, 0))],
        out_specs=pl.BlockSpec((tm, D), lambda i: (i, 0)),
        out_shape=jax.ShapeDtypeStruct((T, D), BF16),
        compiler_params=_params("parallel"),
        name="rmsnorm",
    )(x, g.reshape(1, D))


def rmsnorm_small(x, g, w_small, aux, post, *, hn_dtype=BF16, tm=512, name="rmsnorm_small"):
    T, D = x.shape
    return pl.pallas_call(
        functools.partial(_norm_small_kernel, post=post),
        grid=(T // tm,),
        in_specs=[pl.BlockSpec((tm, D), lambda i: (i, 0)),
                  pl.BlockSpec((1, D), lambda i: (0, 0)),
                  pl.BlockSpec((D, LANES), lambda i: (0, 0)),
                  pl.BlockSpec((8, LANES), lambda i: (0, 0))],
        out_specs=[pl.BlockSpec((tm, D), lambda i: (i, 0)),
                   pl.BlockSpec((tm, LANES), lambda i: (i, 0))],
        out_shape=[jax.ShapeDtypeStruct((T, D), hn_dtype),
                   jax.ShapeDtypeStruct((T, LANES), F32)],
        compiler_params=_params("parallel"),
        name=name,
    )(x, g.reshape(1, D), w_small, aux)


def _pad_cols(w, n=LANES):
    return jnp.pad(w, ((0, 0), (0, n - w.shape[1])))


def _aux_rows(*rows):
    out = [jnp.pad(r.astype(F32), (0, LANES - r.shape[0])) for r in rows]
    out += [jnp.zeros((LANES,), F32)] * (8 - len(out))
    return jnp.stack(out)


def _post_fox(s, aux):
    return _log_sigmoid(s + aux[0:1, :])


def _post_gdn(s, aux, *, hv):
    lane = lax.broadcasted_iota(jnp.int32, s.shape, 1)
    beta = _sigmoid(s)
    g = -jnp.exp(aux[0:1, :]) * _softplus(s + aux[1:2, :])
    return jnp.where(lane < hv, beta, g)


def _post_moe(s, aux):
    del aux
    lane = lax.broadcasted_iota(jnp.int32, s.shape, 1).astype(F32)
    neg = jnp.float32(-jnp.inf)
    logits = jnp.where(lane < N_EXPERTS, s, neg)
    v1 = jnp.max(logits, axis=-1, keepdims=True)
    i1 = jnp.min(jnp.where(logits == v1, lane, float(LANES)), axis=-1, keepdims=True)
    rest = jnp.where(lane == i1, neg, logits)
    v2 = jnp.max(rest, axis=-1, keepdims=True)
    i2 = jnp.min(jnp.where(rest == v2, lane, float(LANES)), axis=-1, keepdims=True)
    e2 = jnp.exp(v2 - v1)
    denom = 1.0 + e2
    out = jnp.where(lane == 0.0, i1, 0.0)
    out = jnp.where(lane == 1.0, i2, out)
    out = jnp.where(lane == 2.0, 1.0 / denom, out)
    out = jnp.where(lane == 3.0, e2 / denom, out)
    return out


def _cumsum_kernel(x_ref, o_ref, *, blk):
    S = x_ref.shape[0]
    r = lax.broadcasted_iota(jnp.int32, (blk, blk), 0)
    c = lax.broadcasted_iota(jnp.int32, (blk, blk), 1)
    tri = (r >= c).astype(F32)
    carry = jnp.zeros((1, LANES), F32)
    for i in range(S // blk):
        cs = jnp.dot(tri, x_ref[i * blk:(i + 1) * blk, :], preferred_element_type=F32,
                     precision=HIGHEST) + carry
        o_ref[i * blk:(i + 1) * blk, :] = cs
        carry = cs[blk - 1:blk, :]


def seq_cumsum(x, batch, *, blk=128):
    T = x.shape[0]
    S = T // batch
    blk = min(blk, S)
    return pl.pallas_call(
        functools.partial(_cumsum_kernel, blk=blk),
        grid=(batch,),
        in_specs=[pl.BlockSpec((S, LANES), lambda b: (b, 0))],
        out_specs=pl.BlockSpec((S, LANES), lambda b: (b, 0)),
        out_shape=jax.ShapeDtypeStruct((T, LANES), F32),
        compiler_params=_params("parallel"),
        name="seq_cumsum",
    )(x)


def _chunk_cumsum_kernel(x_ref, o_ref, *, chunk, lo, hi):
    R = x_ref.shape[0]
    r = lax.broadcasted_iota(jnp.int32, (chunk, chunk), 0)
    c = lax.broadcasted_iota(jnp.int32, (chunk, chunk), 1)
    tri = (r >= c).astype(F32)
    lane = lax.broadcasted_iota(jnp.int32, (chunk, LANES), 1)
    sel = (lane >= lo) & (lane < hi)
    for i in range(R // chunk):
        x = x_ref[i * chunk:(i + 1) * chunk, :]
        cs = jnp.dot(tri, x, preferred_element_type=F32, precision=HIGHEST)
        o_ref[i * chunk:(i + 1) * chunk, :] = jnp.where(sel, cs, x)


def chunk_cumsum(x, *, chunk, lo, hi, tm=512):
    T = x.shape[0]
    tm = min(tm, T)
    return pl.pallas_call(
        functools.partial(_chunk_cumsum_kernel, chunk=chunk, lo=lo, hi=hi),
        grid=(T // tm,),
        in_specs=[pl.BlockSpec((tm, LANES), lambda i: (i, 0))],
        out_specs=pl.BlockSpec((tm, LANES), lambda i: (i, 0)),
        out_shape=jax.ShapeDtypeStruct((T, LANES), F32),
        compiler_params=_params("parallel"),
        name="chunk_cumsum",
    )(x)


def _mm_kernel(a_ref, w_ref, *rest, n_extra, epilogue):
    extra = rest[:n_extra]
    o_ref = rest[n_extra]
    scratch = rest[n_extra + 1:]
    wbf_ref = scratch[0]

    @pl.when(pl.program_id(1) == 0)
    def _():
        wbf_ref[...] = w_ref[...].astype(BF16)

    acc = _dot(a_ref[...], wbf_ref[...])
    epilogue(acc, extra, o_ref, scratch[1:])


def _layer_w_spec(K, tn, layer):
    return pl.BlockSpec((None, K, tn), lambda j, i: (layer, 0, j))


def matmul(a, w, layer, *, n_cols, tm, tn, out_dtype, epilogue, extra=(), extra_specs=(),
           extra_scratch=(), name="matmul"):
    M, K = a.shape
    grid = (n_cols // tn, M // tm)
    return pl.pallas_call(
        functools.partial(_mm_kernel, n_extra=len(extra), epilogue=epilogue),
        grid=grid,
        in_specs=[pl.BlockSpec((tm, K), lambda j, i: (i, 0)),
                  _layer_w_spec(K, tn, layer),
                  *extra_specs],
        out_specs=pl.BlockSpec((tm, tn), lambda j, i: (i, j)),
        out_shape=jax.ShapeDtypeStruct((M, n_cols), out_dtype),
        scratch_shapes=[pltpu.VMEM((K, tn), BF16), *extra_scratch],
        compiler_params=_params("arbitrary", "arbitrary"),
        name=name,
    )(a, w, *extra)


def _epi_residual(acc, extra, o_ref, scratch):
    del scratch
    o_ref[...] = extra[0][...] + acc


def matmul_residual(a, w, layer, res, *, tm, tn, name):
    N = w.shape[2]
    return matmul(a, w, layer, n_cols=N, tm=tm, tn=tn, out_dtype=F32, epilogue=_epi_residual,
                  extra=(res,), extra_specs=(pl.BlockSpec((tm, tn), lambda j, i: (i, j)),),
                  name=name)


def _epi_fox_in(acc, extra, o_ref, scratch, *, n_q_blocks):
    del scratch
    qg_ref, kg_ref = extra
    j = pl.program_id(0)
    tn = acc.shape[1]

    def normed(gain):
        for h in range(tn // HEAD_DIM):
            blk = acc[:, h * HEAD_DIM:(h + 1) * HEAD_DIM]
            o_ref[:, h * HEAD_DIM:(h + 1) * HEAD_DIM] = _rms(blk, gain).astype(o_ref.dtype)

    @pl.when(j < n_q_blocks)
    def _():
        normed(qg_ref[...])

    @pl.when((j >= n_q_blocks) & (j < 2 * n_q_blocks))
    def _():
        normed(kg_ref[...])

    @pl.when(j >= 2 * n_q_blocks)
    def _():
        o_ref[...] = acc.astype(o_ref.dtype)


def _epi_gdn_in(acc, extra, o_ref, scratch, *, n_k_blocks, n_conv_blocks, rows_per_seq, q_scale):
    conv_ref, = extra
    buf_ref, = scratch
    j = pl.program_id(0)
    i = pl.program_id(1)
    tm, tn = acc.shape

    @pl.when(j < n_conv_blocks)
    def _():
        @pl.when(i % rows_per_seq == 0)
        def _():
            buf_ref[0:8, :] = jnp.zeros((8, tn), F32)

        buf_ref[8:8 + tm, :] = acc
        y = acc * conv_ref[GDN_CONV - 1:GDN_CONV, :]
        for tap in range(GDN_CONV - 1):
            shift = GDN_CONV - 1 - tap
            y = y + buf_ref[8 - shift:8 - shift + tm, :] * conv_ref[tap:tap + 1, :]
        buf_ref[0:8, :] = acc[tm - 8:tm, :]
        y = _silu(y)

        @pl.when(j < 2 * n_k_blocks)
        def _():
            scale = jnp.where(j < n_k_blocks, jnp.float32(q_scale), jnp.float32(1.0))
            for h in range(tn // HEAD_DIM):
                blk = y[:, h * HEAD_DIM:(h + 1) * HEAD_DIM]
                ss = jnp.sum(blk * blk, axis=-1, keepdims=True)
                o_ref[:, h * HEAD_DIM:(h + 1) * HEAD_DIM] = (
                    blk * lax.rsqrt(ss + EPS) * scale).astype(o_ref.dtype)

        @pl.when(j >= 2 * n_k_blocks)
        def _():
            o_ref[...] = y.astype(o_ref.dtype)

    @pl.when(j >= n_conv_blocks)
    def _():
        o_ref[...] = acc.astype(o_ref.dtype)


def _swiglu_up_kernel(a_ref, wg_ref, wu_ref, o_ref, wg_bf, wu_bf):
    @pl.when(pl.program_id(1) == 0)
    def _():
        wg_bf[...] = wg_ref[...].astype(BF16)
        wu_bf[...] = wu_ref[...].astype(BF16)

    a = a_ref[...]
    g = _dot(a, wg_bf[...])
    u = _dot(a, wu_bf[...])
    o_ref[...] = (_silu(g) * u).astype(o_ref.dtype)


def swiglu_up(a, wg, wu, layer, *, tm, tn):
    M, K = a.shape
    N = wg.shape[2]
    return pl.pallas_call(
        _swiglu_up_kernel,
        grid=(N // tn, M // tm),
        in_specs=[pl.BlockSpec((tm, K), lambda j, i: (i, 0)),
                  _layer_w_spec(K, tn, layer),
                  _layer_w_spec(K, tn, layer)],
        out_specs=pl.BlockSpec((tm, tn), lambda j, i: (i, j)),
        out_shape=jax.ShapeDtypeStruct((M, N), BF16),
        scratch_shapes=[pltpu.VMEM((K, tn), BF16), pltpu.VMEM((K, tn), BF16)],
        compiler_params=_params("arbitrary", "arbitrary"),
        name="swiglu_up",
    )(a, wg, wu)


def _fox_attn_kernel(q_ref, k_ref, v_ref, og_ref, cq_ref, ck_ref, o_ref,
                     m_ref, l_ref, acc_ref, cqs_ref, *, scale, tq, tk):
    h = pl.program_id(1)
    qi = pl.program_id(2)
    ki = pl.program_id(3)

    @pl.when(ki == 0)
    def _():
        m_ref[...] = jnp.full(m_ref.shape, -jnp.inf, F32)
        l_ref[...] = jnp.zeros(l_ref.shape, F32)
        acc_ref[...] = jnp.zeros(acc_ref.shape, F32)
        lane = lax.broadcasted_iota(jnp.int32, cq_ref.shape, 1)
        cqs_ref[...] = jnp.sum(jnp.where(lane == h, cq_ref[...], 0.0), axis=-1, keepdims=True)

    @pl.when(ki <= qi)
    def _():
        s = _dot_nt(q_ref[...], k_ref[...]) * scale
        s = s + cqs_ref[...] - ck_ref[0]
        row = qi * tq + lax.broadcasted_iota(jnp.int32, (tq, tk), 0)
        col = ki * tk + lax.broadcasted_iota(jnp.int32, (tq, tk), 1)
        s = jnp.where(row >= col, s, -jnp.inf)
        m_prev = m_ref[...]
        m_new = jnp.maximum(m_prev, jnp.max(s, axis=-1, keepdims=True))
        alpha = jnp.exp(m_prev - m_new)
        p = jnp.exp(s - m_new)
        l_ref[...] = alpha * l_ref[...] + jnp.sum(p, axis=-1, keepdims=True)
        acc_ref[...] = alpha * acc_ref[...] + _dot(p.astype(BF16), v_ref[...])
        m_ref[...] = m_new

    @pl.when(ki == qi)
    def _():
        o = acc_ref[...] / l_ref[...]
        o_ref[...] = (o * _sigmoid(og_ref[...].astype(F32))).astype(o_ref.dtype)


def fox_attention(proj, cum, cum_rows, *, batch, heads, tq=512):
    T = proj.shape[0]
    S = T // batch
    tq = min(tq, S)
    tk = tq
    nq = S // tq
    H = heads
    kern = functools.partial(_fox_attn_kernel, scale=HEAD_DIM ** -0.5, tq=tq, tk=tk)
    return pl.pallas_call(
        kern,
        grid=(batch, H, nq, nq),
        in_specs=[
            pl.BlockSpec((tq, HEAD_DIM), lambda b, h, qi, ki: (b * nq + qi, h)),
            pl.BlockSpec((tk, HEAD_DIM), lambda b, h, qi, ki: (b * nq + jnp.minimum(ki, qi), H + h)),
            pl.BlockSpec((tk, HEAD_DIM), lambda b, h, qi, ki: (b * nq + jnp.minimum(ki, qi), 2 * H + h)),
            pl.BlockSpec((tq, HEAD_DIM), lambda b, h, qi, ki: (b * nq + qi, 3 * H + h)),
            pl.BlockSpec((tq, LANES), lambda b, h, qi, ki: (b * nq + qi, 0)),
            pl.BlockSpec((1, 1, tk), lambda b, h, qi, ki: (b * H + h, 0, jnp.minimum(ki, qi))),
        ],
        out_specs=pl.BlockSpec((tq, HEAD_DIM), lambda b, h, qi, ki: (b * nq + qi, h)),
        out_shape=jax.ShapeDtypeStruct((T, H * HEAD_DIM), BF16),
        scratch_shapes=[pltpu.VMEM((tq, 1), F32), pltpu.VMEM((tq, 1), F32),
                        pltpu.VMEM((tq, HEAD_DIM), F32), pltpu.VMEM((tq, 1), F32)],
        compiler_params=_params("parallel", "parallel", "arbitrary", "arbitrary"),
        name="fox_attention",
    )(proj, proj, proj, proj, cum, cum_rows)


def _gdn_kernel(q_ref, k_ref, v_ref, z_ref, small_ref, grow_ref, gain_ref,
                o_ref, gcol_ref, bcol_ref, u_ref, w_ref, a_ref, o_scr,
                *, hv, group):
    C = GDN_CHUNK
    S = q_ref.shape[0]
    n_chunks = S // C
    hk = pl.program_id(1)
    head_cols = [slice(r * HEAD_DIM, (r + 1) * HEAD_DIM) for r in range(2)]

    lane = lax.broadcasted_iota(jnp.int32, small_ref.shape, 1)
    small = small_ref[...]
    for r in range(2):
        head = 2 * hk + r
        bcol_ref[r] = jnp.sum(jnp.where(lane == head, small, 0.0), axis=-1, keepdims=True)
        gcol_ref[r] = jnp.sum(jnp.where(lane == hv + head, small, 0.0), axis=-1, keepdims=True)

    ri = lax.broadcasted_iota(jnp.int32, (C, C), 0)
    ci = lax.broadcasted_iota(jnp.int32, (C, C), 1)
    incl = ri >= ci
    strict = ri > ci
    eye = (ri == ci).astype(F32)

    def prep_chunk(c):
        rows = pl.ds(pl.multiple_of(c * C, C), C)
        kc = k_ref[rows, :]
        qc = q_ref[rows, :]
        kf = kc.astype(F32)
        qkk = _dot_nt(jnp.concatenate([kc, qc], axis=0), kc)
        kk = qkk[:C]
        qk = qkk[C:]
        for r in range(2):
            g = gcol_ref[r, rows, :]
            beta = bcol_ref[r, rows, :]
            grow = grow_ref[r, pl.ds(c, 1), :]
            decay = jnp.exp(jnp.where(incl, g - grow, -jnp.inf))
            lmat = jnp.where(strict, beta * kk * decay, 0.0)
            amat = jnp.where(incl, qk * decay, 0.0)
            x = eye - lmat
            p = _dot(lmat.astype(BF16), lmat.astype(BF16))
            steps = int(math.log2(C)) - 1
            for s in range(steps):
                pb = p.astype(BF16)
                if s < steps - 1:
                    px = _dot(jnp.concatenate([p, x], axis=0).astype(BF16), pb)
                    p, x = px[:C], x + px[C:]
                else:
                    x = x + _dot(x.astype(BF16), pb)
            vf = v_ref[rows, head_cols[r]].astype(F32)
            rhs = jnp.concatenate([vf * beta, kf * (beta * jnp.exp(g))], axis=1)
            sol = _dot(x.astype(BF16), rhs.astype(BF16))
            u_ref[r, rows, :] = sol[:, :HEAD_DIM]
            w_ref[r, rows, :] = sol[:, HEAD_DIM:]
            a_ref[r, rows, :] = amat

    def prep_body(gi, carry):
        for t in range(group):
            prep_chunk(gi * group + t)
        return carry

    lax.fori_loop(0, n_chunks // group, prep_body, 0)

    def rec_body(c, states):
        rows = pl.ds(pl.multiple_of(c * C, C), C)
        kf = k_ref[rows, :].astype(F32)
        qf = q_ref[rows, :].astype(F32)
        new_states = []
        for r in range(2):
            st = states[r]
            g = gcol_ref[r, rows, :]
            g_last = gcol_ref[r, pl.ds(c * C + C - 1, 1), :]
            wq = jnp.concatenate([w_ref[r, rows, :], qf * jnp.exp(g)], axis=0)
            ws_qs = _dot(wq.astype(BF16), st.astype(BF16))
            v_new = u_ref[r, rows, :] - ws_qs[:C]
            vb = v_new.astype(BF16)
            o = ws_qs[C:] + _dot(a_ref[r, rows, :].astype(BF16), vb)
            o_scr[r, rows, :] = o
            k_dec = kf * jnp.exp(g_last - g)
            new_states.append(st * jnp.exp(g_last) + _dot_tn(k_dec.astype(BF16), vb))
        return tuple(new_states)

    zero = jnp.zeros((HEAD_DIM, HEAD_DIM), F32)
    lax.fori_loop(0, n_chunks, rec_body, (zero, zero))

    for r in range(2):
        o = _rms(o_scr[r], gain_ref[...])
        z = z_ref[:, head_cols[r]].astype(F32)
        o_ref[:, head_cols[r]] = (o * _silu(z)).astype(o_ref.dtype)


def gated_delta(proj, small, grow, o_gain, *, batch, k_heads, group=4):
    T = proj.shape[0]
    S = T // batch
    HK = k_heads
    HV = 2 * HK
    C = GDN_CHUNK
    n_chunks = S // C
    group = min(group, n_chunks)
    kern = functools.partial(_gdn_kernel, hv=HV, group=group)
    pair = 2 * HEAD_DIM
    return pl.pallas_call(
        kern,
        grid=(batch, HK),
        in_specs=[
            pl.BlockSpec((S, HEAD_DIM), lambda b, h: (b, h)),
            pl.BlockSpec((S, HEAD_DIM), lambda b, h: (b, HK + h)),
            pl.BlockSpec((S, pair), lambda b, h: (b, HK + h)),
            pl.BlockSpec((S, pair), lambda b, h: (b, 2 * HK + h)),
            pl.BlockSpec((S, LANES), lambda b, h: (b, 0)),
            pl.BlockSpec((2, n_chunks, C), lambda b, h: (b * HK + h, 0, 0)),
            pl.BlockSpec((1, HEAD_DIM), lambda b, h: (0, 0)),
        ],
        out_specs=pl.BlockSpec((S, pair), lambda b, h: (b, h)),
        out_shape=jax.ShapeDtypeStruct((T, HV * HEAD_DIM), BF16),
        scratch_shapes=[pltpu.VMEM((2, S, 1), F32), pltpu.VMEM((2, S, 1), F32),
                        pltpu.VMEM((2, S, HEAD_DIM), F32), pltpu.VMEM((2, S, HEAD_DIM), F32),
                        pltpu.VMEM((2, S, C), F32), pltpu.VMEM((2, S, HEAD_DIM), F32)],
        compiler_params=_params("parallel", "parallel"),
        name="gated_delta",
    )(proj, proj, proj, proj, small, grow, o_gain.reshape(1, HEAD_DIM))


def _row_copy(src_ref, dst_ref, src_row, dst_row, sem):
    return pltpu.make_async_copy(src_ref.at[pl.ds(src_row, 1), :], dst_ref.at[pl.ds(dst_row, 1), :], sem)


def _gather_rows_kernel(idx_ref, src_ref, o_ref, buf_ref, sem, *, rows):
    base = pl.program_id(0) * rows

    def start(r, c):
        _row_copy(src_ref, buf_ref, idx_ref[base + r], r, sem).start()
        return c

    def wait(r, c):
        _row_copy(src_ref, buf_ref, 0, r, sem).wait()
        return c

    lax.fori_loop(0, rows, start, 0)
    lax.fori_loop(0, rows, wait, 0)
    o_ref[...] = buf_ref[...].astype(o_ref.dtype)


def gather_rows(src, idx, *, rows=256):
    P = idx.shape[0]
    D = src.shape[1]
    return pl.pallas_call(
        functools.partial(_gather_rows_kernel, rows=rows),
        grid_spec=pltpu.PrefetchScalarGridSpec(
            num_scalar_prefetch=1,
            grid=(P // rows,),
            in_specs=[pl.BlockSpec(memory_space=pl.ANY)],
            out_specs=pl.BlockSpec((rows, D), lambda t, idx: (t, 0)),
            scratch_shapes=[pltpu.VMEM((rows, D), F32), pltpu.SemaphoreType.DMA(())]),
        out_shape=jax.ShapeDtypeStruct((P, D), BF16),
        compiler_params=_params("arbitrary"),
        name="moe_gather",
    )(idx, src)


def _combine_kernel(pos_ref, y_ref, h_ref, small_ref, o_ref, buf_ref, sem, *, rows):
    base = pl.program_id(0) * rows

    def start(r, c):
        for k in range(2):
            _row_copy(y_ref, buf_ref.at[k], pos_ref[k, base + r], r, sem).start()
        return c

    def wait(r, c):
        for k in range(2):
            _row_copy(y_ref, buf_ref.at[k], 0, r, sem).wait()
        return c

    lax.fori_loop(0, rows, start, 0)
    lax.fori_loop(0, rows, wait, 0)
    w1 = small_ref[:, 2:3]
    w2 = small_ref[:, 3:4]
    o_ref[...] = h_ref[...] + (w1 * buf_ref[0] + w2 * buf_ref[1])


def moe_combine(y, pos, h, small, *, rows=256):
    T, D = h.shape
    return pl.pallas_call(
        functools.partial(_combine_kernel, rows=rows),
        grid_spec=pltpu.PrefetchScalarGridSpec(
            num_scalar_prefetch=1,
            grid=(T // rows,),
            in_specs=[pl.BlockSpec(memory_space=pl.ANY),
                      pl.BlockSpec((rows, D), lambda t, pos: (t, 0)),
                      pl.BlockSpec((rows, LANES), lambda t, pos: (t, 0))],
            out_specs=pl.BlockSpec((rows, D), lambda t, pos: (t, 0)),
            scratch_shapes=[pltpu.VMEM((2, rows, D), F32), pltpu.SemaphoreType.DMA(())]),
        out_shape=jax.ShapeDtypeStruct((T, D), F32),
        compiler_params=_params("arbitrary"),
        name="moe_combine",
    )(pos, y, h, small)


def _expert_up_kernel(te_ref, nv_ref, x_ref, wg_ref, wu_ref, o_ref):
    t = pl.program_id(1)

    @pl.when(t < nv_ref[0])
    def _():
        x = x_ref[...]
        o_ref[...] = (_silu(_dot(x, wg_ref[...])) * _dot(x, wu_ref[...])).astype(o_ref.dtype)

    @pl.when(t >= nv_ref[0])
    def _():
        o_ref[...] = jnp.zeros(o_ref.shape, o_ref.dtype)


def expert_up(x, wg, wu, layer, tile_expert, n_valid, *, tm, tn):
    P, K = x.shape
    N = wg.shape[3]
    last = lambda t, nv: jnp.minimum(t, nv[0] - 1)
    w_spec = pl.BlockSpec((None, None, K, tn), lambda j, t, te, nv: (layer, te[t], 0, j))
    return pl.pallas_call(
        _expert_up_kernel,
        grid_spec=pltpu.PrefetchScalarGridSpec(
            num_scalar_prefetch=2,
            grid=(N // tn, P // tm),
            in_specs=[pl.BlockSpec((tm, K), lambda j, t, te, nv: (last(t, nv), 0)), w_spec, w_spec],
            out_specs=pl.BlockSpec((tm, tn), lambda j, t, te, nv: (t, j))),
        out_shape=jax.ShapeDtypeStruct((P, N), BF16),
        compiler_params=_params("arbitrary", "arbitrary"),
        name="expert_up",
    )(tile_expert, n_valid, x, wg, wu)


def _expert_down_kernel(te_ref, nv_ref, x_ref, w_ref, o_ref):
    t = pl.program_id(1)

    @pl.when(t < nv_ref[0])
    def _():
        o_ref[...] = _dot(x_ref[...], w_ref[...])

    @pl.when(t >= nv_ref[0])
    def _():
        o_ref[...] = jnp.zeros(o_ref.shape, o_ref.dtype)


def expert_down(x, w, layer, tile_expert, n_valid, *, tm, tn):
    P, K = x.shape
    N = w.shape[3]
    last = lambda t, nv: jnp.minimum(t, nv[0] - 1)
    return pl.pallas_call(
        _expert_down_kernel,
        grid_spec=pltpu.PrefetchScalarGridSpec(
            num_scalar_prefetch=2,
            grid=(N // tn, P // tm),
            in_specs=[pl.BlockSpec((tm, K), lambda j, t, te, nv: (last(t, nv), 0)),
                      pl.BlockSpec((None, None, K, tn), lambda j, t, te, nv: (layer, te[t], 0, j))],
            out_specs=pl.BlockSpec((tm, tn), lambda j, t, te, nv: (t, j))),
        out_shape=jax.ShapeDtypeStruct((P, N), F32),
        compiler_params=_params("arbitrary", "arbitrary"),
        name="expert_down",
    )(tile_expert, n_valid, x, w)


def _moe_plan(small, *, tm):
    T = small.shape[0]
    experts = small[:, 0:2].astype(jnp.int32)
    flat = experts.T.reshape(-1)
    onehot = (flat[:, None] == jnp.arange(N_EXPERTS)[None, :]).astype(jnp.int32)
    rank = jnp.take_along_axis(jnp.cumsum(onehot, axis=0) - onehot, flat[:, None], axis=1)[:, 0]
    counts = jnp.sum(onehot, axis=0)
    tiles_per = (counts + tm - 1) // tm
    tile_end = jnp.cumsum(tiles_per)
    tile_start = tile_end - tiles_per
    pos = tile_start[flat] * tm + rank
    n_tiles = (2 * T) // tm + N_EXPERTS
    n_valid = tile_end[-1]
    tile_ids = jnp.minimum(jnp.arange(n_tiles), n_valid - 1)
    tile_expert = jnp.sum((tile_ids[:, None] >= tile_end[None, :]).astype(jnp.int32), axis=1)
    token = jnp.tile(jnp.arange(T, dtype=jnp.int32), 2)
    row_token = jnp.zeros((n_tiles * tm,), jnp.int32).at[pos].set(token)
    return (pos.reshape(2, T).astype(jnp.int32), row_token, tile_expert.astype(jnp.int32),
            n_valid.reshape(1).astype(jnp.int32))


def moe_ffn(h, hn, small, wg, wu, wd, layer, *, tm=512):
    tm = min(tm, h.shape[0])
    pos, row_token, tile_expert, n_valid = _moe_plan(small, tm=tm)
    xs = gather_rows(hn, row_token, rows=min(256, tm))
    N = wg.shape[3]
    tn_up = N // 2 if (N // 2) % LANES == 0 else N
    mid = expert_up(xs, wg, wu, layer, tile_expert, n_valid, tm=tm, tn=tn_up)
    y = expert_down(mid, wd, layer, tile_expert, n_valid, tm=tm, tn=min(1024, wd.shape[3]))
    return moe_combine(y, pos, h, small, rows=min(256, tm))


def _fox_layer(h, batch, j, norm_mix, norm_ffn, w_in, b_f, q_gain, k_gain, w_out, w_gate, w_up,
               w_down):
    T, D = h.shape
    S = T // batch
    H = b_f.shape[0]
    width = H * HEAD_DIM
    hn, lf = rmsnorm_small(h, norm_mix, _pad_cols(w_in[j, :, 4 * width:]), _aux_rows(b_f), _post_fox,
                           name="fox_norm")
    cum = seq_cumsum(lf, batch)
    cum_rows = cum[:, :H].reshape(batch, S, H).transpose(0, 2, 1).reshape(batch * H, 1, S)
    tn = min(1024, width)
    proj = matmul(hn, w_in, j, n_cols=4 * width, tm=min(1024, T), tn=tn, out_dtype=BF16,
                  epilogue=functools.partial(_epi_fox_in, n_q_blocks=width // tn),
                  extra=(q_gain.reshape(1, HEAD_DIM), k_gain.reshape(1, HEAD_DIM)),
                  extra_specs=(pl.BlockSpec((1, HEAD_DIM), lambda j, i: (0, 0)),) * 2,
                  name="fox_in_proj")
    o = fox_attention(proj, cum, cum_rows, batch=batch, heads=H)
    h = matmul_residual(o, w_out, j, h, tm=min(1024, T), tn=min(1024, D), name="fox_out_proj")
    hn2 = rmsnorm(h, norm_ffn)
    F = w_gate.shape[2]
    mid = swiglu_up(hn2, w_gate, w_up, j, tm=min(1024, T), tn=512 if F % 512 == 0 else F)
    return matmul_residual(mid, w_down, j, h, tm=min(512, T), tn=min(512, D), name="ffn_down")


def _gdn_layer(h, batch, j, norm_mix, norm_ffn, w_in, conv_w, a_log, dt_bias, o_gain, w_out,
               router, wg, wu, wd, *, seq_tile=1024):
    T, D = h.shape
    S = T // batch
    HV = a_log.shape[0]
    HK = HV // 2
    kw = HK * HEAD_DIM
    conv_ch = 4 * kw
    main = conv_ch + 2 * kw
    zeros = jnp.zeros((HV,), F32)
    aux = _aux_rows(jnp.concatenate([zeros, a_log]), jnp.concatenate([zeros, dt_bias]))
    hn, small = rmsnorm_small(h, norm_mix, _pad_cols(w_in[j, :, main:]), aux,
                              functools.partial(_post_gdn, hv=HV), name="gdn_norm")
    small = chunk_cumsum(small, chunk=min(GDN_CHUNK, S), lo=HV, hi=2 * HV)
    C = GDN_CHUNK
    grow = small[:, HV:2 * HV].reshape(batch, S, HV).transpose(0, 2, 1).reshape(batch * HV, S // C, C)
    tn = min(1024, kw)
    tm = min(seq_tile, S)
    proj = matmul(hn, w_in, j, n_cols=main, tm=tm, tn=tn, out_dtype=BF16,
                  epilogue=functools.partial(_epi_gdn_in, n_k_blocks=kw // tn,
                                             n_conv_blocks=conv_ch // tn, rows_per_seq=S // tm,
                                             q_scale=HEAD_DIM ** -0.5),
                  extra=(conv_w,),
                  extra_specs=(pl.BlockSpec((GDN_CONV, tn),
                                            lambda j, i: (0, jnp.minimum(j, conv_ch // tn - 1))),),
                  extra_scratch=(pltpu.VMEM((8 + tm, tn), F32),),
                  name="gdn_in_proj")
    o = gated_delta(proj, small, grow, o_gain, batch=batch, k_heads=HK)
    h = matmul_residual(o, w_out, j, h, tm=min(1024, T), tn=min(512, D), name="gdn_out_proj")
    hn2, route = rmsnorm_small(h, norm_ffn, _pad_cols(router), _aux_rows(), _post_moe,
                               hn_dtype=F32, name="moe_norm")
    return moe_ffn(h, hn2, route, wg, wu, wd, j)


def kernel(x, norm_mix, norm_ffn, fox_w_in, fox_b_f, fox_q_norm, fox_k_norm, fox_w_out,
           gdn_w_in, gdn_conv, gdn_a_log, gdn_dt_bias, gdn_o_norm, gdn_w_out,
           ffn_w_gate, ffn_w_up, ffn_w_down,
           moe_router, moe_w_gate, moe_w_up, moe_w_down):
    B, S, D = x.shape
    h = x.reshape(B * S, D)
    depth = norm_mix.shape[0]
    moe_wg, moe_wu, moe_wd = (w.astype(BF16) for w in (moe_w_gate, moe_w_up, moe_w_down))
    for i in range(depth):
        j = i // 2
        if i % 2 == 0:
            h = _fox_layer(h, B, j, norm_mix[i], norm_ffn[i], fox_w_in, fox_b_f[j], fox_q_norm[j],
                           fox_k_norm[j], fox_w_out, ffn_w_gate, ffn_w_up, ffn_w_down)
        else:
            h = _gdn_layer(h, B, j, norm_mix[i], norm_ffn[i], gdn_w_in, gdn_conv[j], gdn_a_log[j],
                           gdn_dt_bias[j], gdn_o_norm[j], gdn_w_out, moe_router[j],
                           moe_wg, moe_wu, moe_wd)
    return h.reshape(B, S, D)
```

```python
import functools
import math

import jax
import jax.numpy as jnp
from jax import lax
from jax.experimental import pallas as pl
from jax.experimental.pallas import tpu as pltpu

F32 = jnp.float32
BF16 = jnp.bfloat16
EPS = 1e-6
LANES = 128
HEAD_DIM = 128
GDN_CHUNK = 64
GDN_CONV = 4
N_EXPERTS = 8
VMEM_LIMIT_BYTES = 56 * 1024 * 1024
HIGHEST = lax.Precision.HIGHEST


def _params(*sem):
    return pltpu.CompilerParams(dimension_semantics=sem, vmem_limit_bytes=VMEM_LIMIT_BYTES)


def _log1pexp_neg_abs(x):
    return jnp.log(1.0 + jnp.exp(-jnp.abs(x)))


def _log_sigmoid(x):
    return jnp.minimum(x, 0.0) - _log1pexp_neg_abs(x)


def _softplus(x):
    return jnp.maximum(x, 0.0) + _log1pexp_neg_abs(x)


def _sigmoid(x):
    return 1.0 / (1.0 + jnp.exp(-x))


def _silu(x):
    return x * _sigmoid(x)


def _dot(a, b):
    return jnp.dot(a, b, preferred_element_type=F32)


def _dot_nt(a, b):
    return lax.dot_general(a, b, (((1,), (1,)), ((), ())), preferred_element_type=F32)


def _dot_tn(a, b):
    return lax.dot_general(a, b, (((0,), (0,)), ((), ())), preferred_element_type=F32)


def _rms(x, g):
    ms = jnp.mean(x * x, axis=-1, keepdims=True)
    return x * lax.rsqrt(ms + EPS) * g


def _norm_kernel(x_ref, g_ref, hn_ref):
    hn_ref[...] = _rms(x_ref[...], g_ref[...]).astype(hn_ref.dtype)


def _norm_small_kernel(x_ref, g_ref, ws_ref, aux_ref, hn_ref, small_ref, *, post):
    y = _rms(x_ref[...], g_ref[...])
    hn_ref[...] = y.astype(hn_ref.dtype)
    s = jnp.dot(y, ws_ref[...], preferred_element_type=F32, precision=HIGHEST)
    small_ref[...] = post(s, aux_ref[...])


def rmsnorm(x, g, *, tm=512):
    T, D = x.shape
    return pl.pallas_call(
        _norm_kernel,
        grid=(T // tm,),
        in_specs=[pl.BlockSpec((tm, D), lambda i: (i, 0)),
                  pl.BlockSpec((1, D), lambda i: (0, 0))],
        out_specs=pl.BlockSpec((tm, D), lambda i: (i, 0)),
        out_shape=jax.ShapeDtypeStruct((T, D), BF16),
        compiler_params=_params("parallel"),
        name="rmsnorm",
    )(x, g.reshape(1, D))


def rmsnorm_small(x, g, w_small, aux, post, *, hn_dtype=BF16, tm=512, name="rmsnorm_small"):
    T, D = x.shape
    return pl.pallas_call(
        functools.partial(_norm_small_kernel, post=post),
        grid=(T // tm,),
        in_specs=[pl.BlockSpec((tm, D), lambda i: (i, 0)),
                  pl.BlockSpec((1, D), lambda i: (0, 0)),
                  pl.BlockSpec((D, LANES), lambda i: (0, 0)),
                  pl.BlockSpec((8, LANES), lambda i: (0, 0))],
        out_specs=[pl.BlockSpec((tm, D), lambda i: (i, 0)),
                   pl.BlockSpec((tm, LANES), lambda i: (i, 0))],
        out_shape=[jax.ShapeDtypeStruct((T, D), hn_dtype),
                   jax.ShapeDtypeStruct((T, LANES), F32)],
        compiler_params=_params("parallel"),
        name=name,
    )(x, g.reshape(1, D), w_small, aux)


def _pad_cols(w, n=LANES):
    return jnp.pad(w, ((0, 0), (0, n - w.shape[1])))


def _aux_rows(*rows):
    out = [jnp.pad(r.astype(F32), (0, LANES - r.shape[0])) for r in rows]
    out += [jnp.zeros((LANES,), F32)] * (8 - len(out))
    return jnp.stack(out)


def _post_fox(s, aux):
    return _log_sigmoid(s + aux[0:1, :])


def _post_gdn(s, aux, *, hv):
    lane = lax.broadcasted_iota(jnp.int32, s.shape, 1)
    beta = _sigmoid(s)
    g = -jnp.exp(aux[0:1, :]) * _softplus(s + aux[1:2, :])
    return jnp.where(lane < hv, beta, g)


def _post_moe(s, aux):
    del aux
    lane = lax.broadcasted_iota(jnp.int32, s.shape, 1).astype(F32)
    neg = jnp.float32(-jnp.inf)
    logits = jnp.where(lane < N_EXPERTS, s, neg)
    v1 = jnp.max(logits, axis=-1, keepdims=True)
    i1 = jnp.min(jnp.where(logits == v1, lane, float(LANES)), axis=-1, keepdims=True)
    rest = jnp.where(lane == i1, neg, logits)
    v2 = jnp.max(rest, axis=-1, keepdims=True)
    i2 = jnp.min(jnp.where(rest == v2, lane, float(LANES)), axis=-1, keepdims=True)
    e2 = jnp.exp(v2 - v1)
    denom = 1.0 + e2
    out = jnp.where(lane == 0.0, i1, 0.0)
    out = jnp.where(lane == 1.0, i2, out)
    out = jnp.where(lane == 2.0, 1.0 / denom, out)
    out = jnp.where(lane == 3.0, e2 / denom, out)
    return out


def _cumsum_kernel(x_ref, o_ref, *, blk):
    S = x_ref.shape[0]
    r = lax.broadcasted_iota(jnp.int32, (blk, blk), 0)
    c = lax.broadcasted_iota(jnp.int32, (blk, blk), 1)
    tri = (r >= c).astype(F32)
    carry = jnp.zeros((1, LANES), F32)
    for i in range(S // blk):
        cs = jnp.dot(tri, x_ref[i * blk:(i + 1) * blk, :], preferred_element_type=F32,
                     precision=HIGHEST) + carry
        o_ref[i * blk:(i + 1) * blk, :] = cs
        carry = cs[blk - 1:blk, :]


def seq_cumsum(x, batch, *, blk=128):
    T = x.shape[0]
    S = T // batch
    blk = min(blk, S)
    return pl.pallas_call(
        functools.partial(_cumsum_kernel, blk=blk),
        grid=(batch,),
        in_specs=[pl.BlockSpec((S, LANES), lambda b: (b, 0))],
        out_specs=pl.BlockSpec((S, LANES), lambda b: (b, 0)),
        out_shape=jax.ShapeDtypeStruct((T, LANES), F32),
        compiler_params=_params("parallel"),
        name="seq_cumsum",
    )(x)


def _chunk_cumsum_kernel(x_ref, o_ref, *, chunk, lo, hi):
    R = x_ref.shape[0]
    r = lax.broadcasted_iota(jnp.int32, (chunk, chunk), 0)
    c = lax.broadcasted_iota(jnp.int32, (chunk, chunk), 1)
    tri = (r >= c).astype(F32)
    lane = lax.broadcasted_iota(jnp.int32, (chunk, LANES), 1)
    sel = (lane >= lo) & (lane < hi)
    for i in range(R // chunk):
        x = x_ref[i * chunk:(i + 1) * chunk, :]
        cs = jnp.dot(tri, x, preferred_element_type=F32, precision=HIGHEST)
        o_ref[i * chunk:(i + 1) * chunk, :] = jnp.where(sel, cs, x)


def chunk_cumsum(x, *, chunk, lo, hi, tm=512):
    T = x.shape[0]
    tm = min(tm, T)
    return pl.pallas_call(
        functools.partial(_chunk_cumsum_kernel, chunk=chunk, lo=lo, hi=hi),
        grid=(T // tm,),
        in_specs=[pl.BlockSpec((tm, LANES), lambda i: (i, 0))],
        out_specs=pl.BlockSpec((tm, LANES), lambda i: (i, 0)),
        out_shape=jax.ShapeDtypeStruct((T, LANES), F32),
        compiler_params=_params("parallel"),
        name="chunk_cumsum",
    )(x)


def _mm_kernel(a_ref, w_ref, *rest, n_extra, epilogue):
    extra = rest[:n_extra]
    o_ref = rest[n_extra]
    scratch = rest[n_extra + 1:]
    wbf_ref = scratch[0]

    @pl.when(pl.program_id(1) == 0)
    def _():
        wbf_ref[...] = w_ref[...].astype(BF16)

    acc = _dot(a_ref[...], wbf_ref[...])
    epilogue(acc, extra, o_ref, scratch[1:])


def _layer_w_spec(K, tn, layer):
    return pl.BlockSpec((None, K, tn), lambda j, i: (layer, 0, j))


def matmul(a, w, layer, *, n_cols, tm, tn, out_dtype, epilogue, extra=(), extra_specs=(),
           extra_scratch=(), name="matmul"):
    M, K = a.shape
    grid = (n_cols // tn, M // tm)
    return pl.pallas_call(
        functools.partial(_mm_kernel, n_extra=len(extra), epilogue=epilogue),
        grid=grid,
        in_specs=[pl.BlockSpec((tm, K), lambda j, i: (i, 0)),
                  _layer_w_spec(K, tn, layer),
                  *extra_specs],
        out_specs=pl.BlockSpec((tm, tn), lambda j, i: (i, j)),
        out_shape=jax.ShapeDtypeStruct((M, n_cols), out_dtype),
        scratch_shapes=[pltpu.VMEM((K, tn), BF16), *extra_scratch],
        compiler_params=_params("arbitrary", "arbitrary"),
        name=name,
    )(a, w, *extra)


def _epi_residual(acc, extra, o_ref, scratch):
    del scratch
    o_ref[...] = extra[0][...] + acc


def matmul_residual(a, w, layer, res, *, tm, tn, name):
    N = w.shape[2]
    return matmul(a, w, layer, n_cols=N, tm=tm, tn=tn, out_dtype=F32, epilogue=_epi_residual,
                  extra=(res,), extra_specs=(pl.BlockSpec((tm, tn), lambda j, i: (i, j)),),
                  name=name)


def _epi_fox_in(acc, extra, o_ref, scratch, *, n_q_blocks):
    del scratch
    qg_ref, kg_ref = extra
    j = pl.program_id(0)
    tn = acc.shape[1]

    def normed(gain):
        for h in range(tn // HEAD_DIM):
            blk = acc[:, h * HEAD_DIM:(h + 1) * HEAD_DIM]
            o_ref[:, h * HEAD_DIM:(h + 1) * HEAD_DIM] = _rms(blk, gain).astype(o_ref.dtype)

    @pl.when(j < n_q_blocks)
    def _():
        normed(qg_ref[...])

    @pl.when((j >= n_q_blocks) & (j < 2 * n_q_blocks))
    def _():
        normed(kg_ref[...])

    @pl.when(j >= 2 * n_q_blocks)
    def _():
        o_ref[...] = acc.astype(o_ref.dtype)


def _epi_gdn_in(acc, extra, o_ref, scratch, *, n_k_blocks, n_conv_blocks, rows_per_seq, q_scale):
    conv_ref, = extra
    buf_ref, = scratch
    j = pl.program_id(0)
    i = pl.program_id(1)
    tm, tn = acc.shape

    @pl.when(j < n_conv_blocks)
    def _():
        @pl.when(i % rows_per_seq == 0)
        def _():
            buf_ref[0:8, :] = jnp.zeros((8, tn), F32)

        buf_ref[8:8 + tm, :] = acc
        y = acc * conv_ref[GDN_CONV - 1:GDN_CONV, :]
        for tap in range(GDN_CONV - 1):
            shift = GDN_CONV - 1 - tap
            y = y + buf_ref[8 - shift:8 - shift + tm, :] * conv_ref[tap:tap + 1, :]
        buf_ref[0:8, :] = acc[tm - 8:tm, :]
        y = _silu(y)

        @pl.when(j < 2 * n_k_blocks)
        def _():
            scale = jnp.where(j < n_k_blocks, jnp.float32(q_scale), jnp.float32(1.0))
            for h in range(tn // HEAD_DIM):
                blk = y[:, h * HEAD_DIM:(h + 1) * HEAD_DIM]
                ss = jnp.sum(blk * blk, axis=-1, keepdims=True)
                o_ref[:, h * HEAD_DIM:(h + 1) * HEAD_DIM] = (
                    blk * lax.rsqrt(ss + EPS) * scale).astype(o_ref.dtype)

        @pl.when(j >= 2 * n_k_blocks)
        def _():
            o_ref[...] = y.astype(o_ref.dtype)

    @pl.when(j >= n_conv_blocks)
    def _():
        o_ref[...] = acc.astype(o_ref.dtype)


def _swiglu_up_kernel(a_ref, wg_ref, wu_ref, o_ref, wg_bf, wu_bf):
    @pl.when(pl.program_id(1) == 0)
    def _():
        wg_bf[...] = wg_ref[...].astype(BF16)
        wu_bf[...] = wu_ref[...].astype(BF16)

    a = a_ref[...]
    g = _dot(a, wg_bf[...])
    u = _dot(a, wu_bf[...])
    o_ref[...] = (_silu(g) * u).astype(o_ref.dtype)


def swiglu_up(a, wg, wu, layer, *, tm, tn):
    M, K = a.shape
    N = wg.shape[2]
    return pl.pallas_call(
        _swiglu_up_kernel,
        grid=(N // tn, M // tm),
        in_specs=[pl.BlockSpec((tm, K), lambda j, i: (i, 0)),
                  _layer_w_spec(K, tn, layer),
                  _layer_w_spec(K, tn, layer)],
        out_specs=pl.BlockSpec((tm, tn), lambda j, i: (i, j)),
        out_shape=jax.ShapeDtypeStruct((M, N), BF16),
        scratch_shapes=[pltpu.VMEM((K, tn), BF16), pltpu.VMEM((K, tn), BF16)],
        compiler_params=_params("arbitrary", "arbitrary"),
        name="swiglu_up",
    )(a, wg, wu)


def _fox_attn_kernel(q_ref, k_ref, v_ref, og_ref, cq_ref, ck_ref, o_ref,
                     m_ref, l_ref, acc_ref, cqs_ref, *, scale, tq, tk):
    h = pl.program_id(1)
    qi = pl.program_id(2)
    ki = pl.program_id(3)

    @pl.when(ki == 0)
    def _():
        m_ref[...] = jnp.full(m_ref.shape, -jnp.inf, F32)
        l_ref[...] = jnp.zeros(l_ref.shape, F32)
        acc_ref[...] = jnp.zeros(acc_ref.shape, F32)
        lane = lax.broadcasted_iota(jnp.int32, cq_ref.shape, 1)
        cqs_ref[...] = jnp.sum(jnp.where(lane == h, cq_ref[...], 0.0), axis=-1, keepdims=True)

    @pl.when(ki <= qi)
    def _():
        s = _dot_nt(q_ref[...], k_ref[...]) * scale
        s = s + cqs_ref[...] - ck_ref[0]
        row = qi * tq + lax.broadcasted_iota(jnp.int32, (tq, tk), 0)
        col = ki * tk + lax.broadcasted_iota(jnp.int32, (tq, tk), 1)
        s = jnp.where(row >= col, s, -jnp.inf)
        m_prev = m_ref[...]
        m_new = jnp.maximum(m_prev, jnp.max(s, axis=-1, keepdims=True))
        alpha = jnp.exp(m_prev - m_new)
        p = jnp.exp(s - m_new)
        l_ref[...] = alpha * l_ref[...] + jnp.sum(p, axis=-1, keepdims=True)
        acc_ref[...] = alpha * acc_ref[...] + _dot(p.astype(BF16), v_ref[...])
        m_ref[...] = m_new

    @pl.when(ki == qi)
    def _():
        o = acc_ref[...] / l_ref[...]
        o_ref[...] = (o * _sigmoid(og_ref[...].astype(F32))).astype(o_ref.dtype)


def fox_attention(proj, cum, cum_rows, *, batch, heads, tq=512):
    T = proj.shape[0]
    S = T // batch
    tq = min(tq, S)
    tk = tq
    nq = S // tq
    H = heads
    kern = functools.partial(_fox_attn_kernel, scale=HEAD_DIM ** -0.5, tq=tq, tk=tk)
    return pl.pallas_call(
        kern,
        grid=(batch, H, nq, nq),
        in_specs=[
            pl.BlockSpec((tq, HEAD_DIM), lambda b, h, qi, ki: (b * nq + qi, h)),
            pl.BlockSpec((tk, HEAD_DIM), lambda b, h, qi, ki: (b * nq + jnp.minimum(ki, qi), H + h)),
            pl.BlockSpec((tk, HEAD_DIM), lambda b, h, qi, ki: (b * nq + jnp.minimum(ki, qi), 2 * H + h)),
            pl.BlockSpec((tq, HEAD_DIM), lambda b, h, qi, ki: (b * nq + qi, 3 * H + h)),
            pl.BlockSpec((tq, LANES), lambda b, h, qi, ki: (b * nq + qi, 0)),
            pl.BlockSpec((1, 1, tk), lambda b, h, qi, ki: (b * H + h, 0, jnp.minimum(ki, qi))),
        ],
        out_specs=pl.BlockSpec((tq, HEAD_DIM), lambda b, h, qi, ki: (b * nq + qi, h)),
        out_shape=jax.ShapeDtypeStruct((T, H * HEAD_DIM), BF16),
        scratch_shapes=[pltpu.VMEM((tq, 1), F32), pltpu.VMEM((tq, 1), F32),
                        pltpu.VMEM((tq, HEAD_DIM), F32), pltpu.VMEM((tq, 1), F32)],
        compiler_params=_params("parallel", "parallel", "arbitrary", "arbitrary"),
        name="fox_attention",
    )(proj, proj, proj, proj, cum, cum_rows)


def _gdn_kernel(q_ref, k_ref, v_ref, z_ref, small_ref, grow_ref, gain_ref,
                o_ref, cols_ref, u_ref, w_ref, a_ref, o_scr, *, hv, nk, group):
    C = GDN_CHUNK
    S = q_ref.shape[0]
    n_chunks = S // C
    nv = 2 * nk
    hp = pl.program_id(1)
    cols = lambda i: slice(i * HEAD_DIM, (i + 1) * HEAD_DIM)

    cols_ref[...] = pltpu.roll(small_ref[...], (LANES - nv * hp) % LANES, 1)

    ri = lax.broadcasted_iota(jnp.int32, (C, C), 0)
    ci = lax.broadcasted_iota(jnp.int32, (C, C), 1)
    incl = ri >= ci
    strict = ri > ci
    eye = (ri == ci).astype(F32)
    steps = int(math.log2(C)) - 1

    def prep_stages(gi):
        chunk_rows = [pl.ds(pl.multiple_of((gi * group + t) * C, C), C) for t in range(group)]
        kf, qkk = {}, {}
        for t, rows in enumerate(chunk_rows):
            for kh in range(nk):
                kc = k_ref[rows, cols(kh)]
                qc = q_ref[rows, cols(kh)]
                kf[t, kh] = kc.astype(F32)
                qkk[t, kh] = _dot_nt(jnp.concatenate([kc, qc], axis=0), kc)
        yield
        chains = [(t, h) for t in range(group) for h in range(nv)]
        beta, g, x, lb = {}, {}, {}, {}
        for t, h in chains:
            rows = chunk_rows[t]
            g[t, h] = cols_ref[rows, hv + h:hv + h + 1]
            beta[t, h] = cols_ref[rows, h:h + 1]
            grow = grow_ref[h, pl.ds(gi * group + t, 1), :]
            decay = jnp.exp(jnp.where(incl, g[t, h] - grow, -jnp.inf))
            lmat = jnp.where(strict, beta[t, h] * qkk[t, h // 2][:C] * decay, 0.0)
            a_ref[h, rows, :] = jnp.where(incl, qkk[t, h // 2][C:] * decay, 0.0)
            x[t, h] = eye - lmat
            lb[t, h] = lmat.astype(BF16)
        p = {ch: _dot(lb[ch], lb[ch]) for ch in chains}
        yield
        for s in range(steps):
            if s < steps - 1:
                px = {ch: _dot(jnp.concatenate([p[ch], x[ch]], axis=0).astype(BF16), p[ch].astype(BF16))
                      for ch in chains}
                for ch in chains:
                    p[ch], x[ch] = px[ch][:C], x[ch] + px[ch][C:]
            else:
                xp = {ch: _dot(x[ch].astype(BF16), p[ch].astype(BF16)) for ch in chains}
                for ch in chains:
                    x[ch] = x[ch] + xp[ch]
            yield
        sol = {}
        for t, h in chains:
            vf = v_ref[chunk_rows[t], cols(h)].astype(F32)
            rhs = jnp.concatenate([vf * beta[t, h], kf[t, h // 2] * (beta[t, h] * jnp.exp(g[t, h]))], axis=1)
            sol[t, h] = _dot(x[t, h].astype(BF16), rhs.astype(BF16))
        for t, h in chains:
            u_ref[h, chunk_rows[t], :] = sol[t, h][:, :HEAD_DIM]
            w_ref[h, chunk_rows[t], :] = sol[t, h][:, HEAD_DIM:]
        yield

    def rec_stages(gi, states):
        for t in range(group):
            c = gi * group + t
            rows = pl.ds(pl.multiple_of(c * C, C), C)
            last = pl.ds(c * C + C - 1, 1)
            kf = [k_ref[rows, cols(kh)].astype(F32) for kh in range(nk)]
            qf = [q_ref[rows, cols(kh)].astype(F32) for kh in range(nk)]
            g = [cols_ref[rows, hv + h:hv + h + 1] for h in range(nv)]
            g_last = [cols_ref[last, hv + h:hv + h + 1] for h in range(nv)]
            ws_qs = []
            for h in range(nv):
                wq = jnp.concatenate([w_ref[h, rows, :], qf[h // 2] * jnp.exp(g[h])], axis=0)
                ws_qs.append(_dot(wq.astype(BF16), states[h].astype(BF16)))
            yield
            vb = [(u_ref[h, rows, :] - ws_qs[h][:C]).astype(BF16) for h in range(nv)]
            av = [_dot(a_ref[h, rows, :].astype(BF16), vb[h]) for h in range(nv)]
            kv = [_dot_tn((kf[h // 2] * jnp.exp(g_last[h] - g[h])).astype(BF16), vb[h]) for h in range(nv)]
            for h in range(nv):
                o_scr[h, rows, :] = ws_qs[h][C:] + av[h]
                states[h] = states[h] * jnp.exp(g_last[h]) + kv[h]
            yield

    def run_interleaved(*gens):
        live = list(gens)
        while live:
            for gen in list(live):
                if next(gen, "done") == "done":
                    live.remove(gen)

    n_groups = n_chunks // group
    run_interleaved(prep_stages(0))

    def body(gi, states):
        states = list(states)
        run_interleaved(rec_stages(gi, states), prep_stages(gi + 1))
        return tuple(states)

    zero = jnp.zeros((HEAD_DIM, HEAD_DIM), F32)
    states = list(lax.fori_loop(0, n_groups - 1, body, (zero,) * nv))
    run_interleaved(rec_stages(n_groups - 1, states))

    for h in range(nv):
        o = _rms(o_scr[h], gain_ref[...])
        z = z_ref[:, cols(h)].astype(F32)
        o_ref[:, cols(h)] = (o * _silu(z)).astype(o_ref.dtype)


def gated_delta(proj, small, grow, o_gain, *, batch, k_heads, nk=2, group=4):
    T = proj.shape[0]
    S = T // batch
    HK = k_heads
    HV = 2 * HK
    C = GDN_CHUNK
    n_chunks = S // C
    nk = min(nk, HK)
    nv = 2 * nk
    NP = HK // nk
    group = min(group, n_chunks)
    kern = functools.partial(_gdn_kernel, hv=HV, nk=nk, group=group)
    kw = nk * HEAD_DIM
    vw = nv * HEAD_DIM
    return pl.pallas_call(
        kern,
        grid=(batch, NP),
        in_specs=[
            pl.BlockSpec((S, kw), lambda b, h: (b, h)),
            pl.BlockSpec((S, kw), lambda b, h: (b, NP + h)),
            pl.BlockSpec((S, vw), lambda b, h: (b, NP + h)),
            pl.BlockSpec((S, vw), lambda b, h: (b, 2 * NP + h)),
            pl.BlockSpec((S, LANES), lambda b, h: (b, 0)),
            pl.BlockSpec((nv, n_chunks, C), lambda b, h: (b * NP + h, 0, 0)),
            pl.BlockSpec((1, HEAD_DIM), lambda b, h: (0, 0)),
        ],
        out_specs=pl.BlockSpec((S, vw), lambda b, h: (b, h)),
        out_shape=jax.ShapeDtypeStruct((T, HV * HEAD_DIM), BF16),
        scratch_shapes=[pltpu.VMEM((S, LANES), F32),
                        pltpu.VMEM((nv, S, HEAD_DIM), F32), pltpu.VMEM((nv, S, HEAD_DIM), F32),
                        pltpu.VMEM((nv, S, C), F32), pltpu.VMEM((nv, S, HEAD_DIM), F32)],
        compiler_params=_params("parallel", "parallel"),
        name="gated_delta",
    )(proj, proj, proj, proj, small, grow, o_gain.reshape(1, HEAD_DIM))


def _row_copy(src_ref, dst_ref, src_row, dst_row, sem):
    return pltpu.make_async_copy(src_ref.at[pl.ds(src_row, 1), :], dst_ref.at[pl.ds(dst_row, 1), :], sem)


def _gather_rows_kernel(idx_ref, src_ref, o_ref, buf_ref, sem, *, rows):
    base = pl.program_id(0) * rows

    def start(r, c):
        _row_copy(src_ref, buf_ref, idx_ref[base + r], r, sem).start()
        return c

    def wait(r, c):
        _row_copy(src_ref, buf_ref, 0, r, sem).wait()
        return c

    lax.fori_loop(0, rows, start, 0)
    lax.fori_loop(0, rows, wait, 0)
    o_ref[...] = buf_ref[...].astype(o_ref.dtype)


def gather_rows(src, idx, *, rows=256):
    P = idx.shape[0]
    D = src.shape[1]
    return pl.pallas_call(
        functools.partial(_gather_rows_kernel, rows=rows),
        grid_spec=pltpu.PrefetchScalarGridSpec(
            num_scalar_prefetch=1,
            grid=(P // rows,),
            in_specs=[pl.BlockSpec(memory_space=pl.ANY)],
            out_specs=pl.BlockSpec((rows, D), lambda t, idx: (t, 0)),
            scratch_shapes=[pltpu.VMEM((rows, D), F32), pltpu.SemaphoreType.DMA(())]),
        out_shape=jax.ShapeDtypeStruct((P, D), BF16),
        compiler_params=_params("arbitrary"),
        name="moe_gather",
    )(idx, src)


def _combine_kernel(pos_ref, y_ref, h_ref, small_ref, o_ref, buf_ref, sem, *, rows):
    base = pl.program_id(0) * rows

    def start(r, c):
        for k in range(2):
            _row_copy(y_ref, buf_ref.at[k], pos_ref[k, base + r], r, sem).start()
        return c

    def wait(r, c):
        for k in range(2):
            _row_copy(y_ref, buf_ref.at[k], 0, r, sem).wait()
        return c

    lax.fori_loop(0, rows, start, 0)
    lax.fori_loop(0, rows, wait, 0)
    w1 = small_ref[:, 2:3]
    w2 = small_ref[:, 3:4]
    o_ref[...] = h_ref[...] + (w1 * buf_ref[0] + w2 * buf_ref[1])


def moe_combine(y, pos, h, small, *, rows=256):
    T, D = h.shape
    return pl.pallas_call(
        functools.partial(_combine_kernel, rows=rows),
        grid_spec=pltpu.PrefetchScalarGridSpec(
            num_scalar_prefetch=1,
            grid=(T // rows,),
            in_specs=[pl.BlockSpec(memory_space=pl.ANY),
                      pl.BlockSpec((rows, D), lambda t, pos: (t, 0)),
                      pl.BlockSpec((rows, LANES), lambda t, pos: (t, 0))],
            out_specs=pl.BlockSpec((rows, D), lambda t, pos: (t, 0)),
            scratch_shapes=[pltpu.VMEM((2, rows, D), F32), pltpu.SemaphoreType.DMA(())]),
        out_shape=jax.ShapeDtypeStruct((T, D), F32),
        compiler_params=_params("arbitrary"),
        name="moe_combine",
    )(pos, y, h, small)


def _expert_up_kernel(te_ref, nv_ref, x_ref, wg_ref, wu_ref, o_ref):
    t = pl.program_id(1)

    @pl.when(t < nv_ref[0])
    def _():
        x = x_ref[...]
        o_ref[...] = (_silu(_dot(x, wg_ref[...])) * _dot(x, wu_ref[...])).astype(o_ref.dtype)

    @pl.when(t >= nv_ref[0])
    def _():
        o_ref[...] = jnp.zeros(o_ref.shape, o_ref.dtype)


def expert_up(x, wg, wu, layer, tile_expert, n_valid, *, tm, tn):
    P, K = x.shape
    N = wg.shape[3]
    last = lambda t, nv: jnp.minimum(t, nv[0] - 1)
    w_spec = pl.BlockSpec((None, None, K, tn), lambda j, t, te, nv: (layer, te[t], 0, j))
    return pl.pallas_call(
        _expert_up_kernel,
        grid_spec=pltpu.PrefetchScalarGridSpec(
            num_scalar_prefetch=2,
            grid=(N // tn, P // tm),
            in_specs=[pl.BlockSpec((tm, K), lambda j, t, te, nv: (last(t, nv), 0)), w_spec, w_spec],
            out_specs=pl.BlockSpec((tm, tn), lambda j, t, te, nv: (t, j))),
        out_shape=jax.ShapeDtypeStruct((P, N), BF16),
        compiler_params=_params("arbitrary", "arbitrary"),
        name="expert_up",
    )(tile_expert, n_valid, x, wg, wu)


def _expert_down_kernel(te_ref, nv_ref, x_ref, w_ref, o_ref):
    t = pl.program_id(1)

    @pl.when(t < nv_ref[0])
    def _():
        o_ref[...] = _dot(x_ref[...], w_ref[...])

    @pl.when(t >= nv_ref[0])
    def _():
        o_ref[...] = jnp.zeros(o_ref.shape, o_ref.dtype)


def expert_down(x, w, layer, tile_expert, n_valid, *, tm, tn):
    P, K = x.shape
    N = w.shape[3]
    last = lambda t, nv: jnp.minimum(t, nv[0] - 1)
    return pl.pallas_call(
        _expert_down_kernel,
        grid_spec=pltpu.PrefetchScalarGridSpec(
            num_scalar_prefetch=2,
            grid=(N // tn, P // tm),
            in_specs=[pl.BlockSpec((tm, K), lambda j, t, te, nv: (last(t, nv), 0)),
                      pl.BlockSpec((None, None, K, tn), lambda j, t, te, nv: (layer, te[t], 0, j))],
            out_specs=pl.BlockSpec((tm, tn), lambda j, t, te, nv: (t, j))),
        out_shape=jax.ShapeDtypeStruct((P, N), F32),
        compiler_params=_params("arbitrary", "arbitrary"),
        name="expert_down",
    )(tile_expert, n_valid, x, w)


def _moe_plan(small, *, tm):
    T = small.shape[0]
    experts = small[:, 0:2].astype(jnp.int32)
    flat = experts.T.reshape(-1)
    onehot = (flat[:, None] == jnp.arange(N_EXPERTS)[None, :]).astype(jnp.int32)
    rank = jnp.take_along_axis(jnp.cumsum(onehot, axis=0) - onehot, flat[:, None], axis=1)[:, 0]
    counts = jnp.sum(onehot, axis=0)
    tiles_per = (counts + tm - 1) // tm
    tile_end = jnp.cumsum(tiles_per)
    tile_start = tile_end - tiles_per
    pos = tile_start[flat] * tm + rank
    n_tiles = (2 * T) // tm + N_EXPERTS
    n_valid = tile_end[-1]
    tile_ids = jnp.minimum(jnp.arange(n_tiles), n_valid - 1)
    tile_expert = jnp.sum((tile_ids[:, None] >= tile_end[None, :]).astype(jnp.int32), axis=1)
    token = jnp.tile(jnp.arange(T, dtype=jnp.int32), 2)
    row_token = jnp.zeros((n_tiles * tm,), jnp.int32).at[pos].set(token)
    return (pos.reshape(2, T).astype(jnp.int32), row_token, tile_expert.astype(jnp.int32),
            n_valid.reshape(1).astype(jnp.int32))


def moe_ffn(h, hn, small, wg, wu, wd, layer, *, tm=512):
    tm = min(tm, h.shape[0])
    pos, row_token, tile_expert, n_valid = _moe_plan(small, tm=tm)
    xs = gather_rows(hn, row_token, rows=min(256, tm))
    N = wg.shape[3]
    tn_up = N // 2 if (N // 2) % LANES == 0 else N
    mid = expert_up(xs, wg, wu, layer, tile_expert, n_valid, tm=tm, tn=tn_up)
    y = expert_down(mid, wd, layer, tile_expert, n_valid, tm=tm, tn=min(1024, wd.shape[3]))
    return moe_combine(y, pos, h, small, rows=min(256, tm))


def _fox_layer(h, batch, j, norm_mix, norm_ffn, w_in, b_f, q_gain, k_gain, w_out, w_gate, w_up,
               w_down):
    T, D = h.shape
    S = T // batch
    H = b_f.shape[0]
    width = H * HEAD_DIM
    hn, lf = rmsnorm_small(h, norm_mix, _pad_cols(w_in[j, :, 4 * width:]), _aux_rows(b_f), _post_fox,
                           name="fox_norm")
    cum = seq_cumsum(lf, batch)
    cum_rows = cum[:, :H].reshape(batch, S, H).transpose(0, 2, 1).reshape(batch * H, 1, S)
    tn = min(1024, width)
    proj = matmul(hn, w_in, j, n_cols=4 * width, tm=min(1024, T), tn=tn, out_dtype=BF16,
                  epilogue=functools.partial(_epi_fox_in, n_q_blocks=width // tn),
                  extra=(q_gain.reshape(1, HEAD_DIM), k_gain.reshape(1, HEAD_DIM)),
                  extra_specs=(pl.BlockSpec((1, HEAD_DIM), lambda j, i: (0, 0)),) * 2,
                  name="fox_in_proj")
    o = fox_attention(proj, cum, cum_rows, batch=batch, heads=H)
    h = matmul_residual(o, w_out, j, h, tm=min(1024, T), tn=min(1024, D), name="fox_out_proj")
    hn2 = rmsnorm(h, norm_ffn)
    F = w_gate.shape[2]
    mid = swiglu_up(hn2, w_gate, w_up, j, tm=min(1024, T), tn=512 if F % 512 == 0 else F)
    return matmul_residual(mid, w_down, j, h, tm=min(512, T), tn=min(512, D), name="ffn_down")


def _gdn_layer(h, batch, j, norm_mix, norm_ffn, w_in, conv_w, a_log, dt_bias, o_gain, w_out,
               router, wg, wu, wd, *, seq_tile=1024):
    T, D = h.shape
    S = T // batch
    HV = a_log.shape[0]
    HK = HV // 2
    kw = HK * HEAD_DIM
    conv_ch = 4 * kw
    main = conv_ch + 2 * kw
    zeros = jnp.zeros((HV,), F32)
    aux = _aux_rows(jnp.concatenate([zeros, a_log]), jnp.concatenate([zeros, dt_bias]))
    hn, small = rmsnorm_small(h, norm_mix, _pad_cols(w_in[j, :, main:]), aux,
                              functools.partial(_post_gdn, hv=HV), name="gdn_norm")
    small = chunk_cumsum(small, chunk=min(GDN_CHUNK, S), lo=HV, hi=2 * HV)
    C = GDN_CHUNK
    grow = small[:, HV:2 * HV].reshape(batch, S, HV).transpose(0, 2, 1).reshape(batch * HV, S // C, C)
    tn = min(1024, kw)
    tm = min(seq_tile, S)
    proj = matmul(hn, w_in, j, n_cols=main, tm=tm, tn=tn, out_dtype=BF16,
                  epilogue=functools.partial(_epi_gdn_in, n_k_blocks=kw // tn,
                                             n_conv_blocks=conv_ch // tn, rows_per_seq=S // tm,
                                             q_scale=HEAD_DIM ** -0.5),
                  extra=(conv_w,),
                  extra_specs=(pl.BlockSpec((GDN_CONV, tn),
                                            lambda j, i: (0, jnp.minimum(j, conv_ch // tn - 1))),),
                  extra_scratch=(pltpu.VMEM((8 + tm, tn), F32),),
                  name="gdn_in_proj")
    o = gated_delta(proj, small, grow, o_gain, batch=batch, k_heads=HK)
    h = matmul_residual(o, w_out, j, h, tm=min(1024, T), tn=min(512, D), name="gdn_out_proj")
    hn2, route = rmsnorm_small(h, norm_ffn, _pad_cols(router), _aux_rows(), _post_moe,
                               hn_dtype=F32, name="moe_norm")
    return moe_ffn(h, hn2, route, wg, wu, wd, j)


def kernel(x, norm_mix, norm_ffn, fox_w_in, fox_b_f, fox_q_norm, fox_k_norm, fox_w_out,
           gdn_w_in, gdn_conv, gdn_a_log, gdn_dt_bias, gdn_o_norm, gdn_w_out,
           ffn_w_gate, ffn_w_up, ffn_w_down,
           moe_router, moe_w_gate, moe_w_up, moe_w_down):
    B, S, D = x.shape
    h = x.reshape(B * S, D)
    depth = norm_mix.shape[0]
    moe_wg, moe_wu, moe_wd = (w.astype(BF16) for w in (moe_w_gate, moe_w_up, moe_w_down))
    for i in range(depth):
        j = i // 2
        if i % 2 == 0:
            h = _fox_layer(h, B, j, norm_mix[i], norm_ffn[i], fox_w_in, fox_b_f[j], fox_q_norm[j],
                           fox_k_norm[j], fox_w_out, ffn_w_gate, ffn_w_up, ffn_w_down)
        else:
            h = _gdn_layer(h, B, j, norm_mix[i], norm_ffn[i], gdn_w_in, gdn_conv[j], gdn_a_log[j],
                           gdn_dt_bias[j], gdn_o_norm[j], gdn_w_out, moe_router[j],
                           moe_wg, moe_wu, moe_wd)
    return h.reshape(B, S, D)
```

```python
import functools
import math

import jax
import jax.numpy as jnp
from jax import lax
from jax.experimental import pallas as pl
from jax.experimental.pallas import tpu as pltpu

F32 = jnp.float32
BF16 = jnp.bfloat16
EPS = 1e-6
LANES = 128
HEAD_DIM = 128
GDN_CHUNK = 64
GDN_CONV = 4
N_EXPERTS = 8
VMEM_LIMIT_BYTES = 56 * 1024 * 1024
HIGHEST = lax.Precision.HIGHEST
LOG2E = math.log2(math.e)


def _params(*sem):
    return pltpu.CompilerParams(dimension_semantics=sem, vmem_limit_bytes=VMEM_LIMIT_BYTES)


def _log1pexp_neg_abs(x):
    return jnp.log(1.0 + jnp.exp(-jnp.abs(x)))


def _log_sigmoid(x):
    return jnp.minimum(x, 0.0) - _log1pexp_neg_abs(x)


def _softplus(x):
    return jnp.maximum(x, 0.0) + _log1pexp_neg_abs(x)


def _sigmoid(x):
    return 1.0 / (1.0 + jnp.exp(-x))


def _silu(x):
    return x * _sigmoid(x)


def _dot(a, b):
    return jnp.dot(a, b, preferred_element_type=F32)


def _dot_nt(a, b):
    return lax.dot_general(a, b, (((1,), (1,)), ((), ())), preferred_element_type=F32)


def _dot_tn(a, b):
    return lax.dot_general(a, b, (((0,), (0,)), ((), ())), preferred_element_type=F32)


def _rms(x, g):
    ms = jnp.mean(x * x, axis=-1, keepdims=True)
    return x * lax.rsqrt(ms + EPS) * g


def _norm_kernel(x_ref, g_ref, hn_ref):
    hn_ref[...] = _rms(x_ref[...], g_ref[...]).astype(hn_ref.dtype)


def _norm_small_kernel(x_ref, g_ref, ws_ref, aux_ref, hn_ref, small_ref, *, post):
    y = _rms(x_ref[...], g_ref[...])
    hn_ref[...] = y.astype(hn_ref.dtype)
    s = jnp.dot(y, ws_ref[...], preferred_element_type=F32, precision=HIGHEST)
    small_ref[...] = post(s, aux_ref[...])


def rmsnorm(x, g, *, tm=512):
    T, D = x.shape
    return pl.pallas_call(
        _norm_kernel,
        grid=(T // tm,),
        in_specs=[pl.BlockSpec((tm, D), lambda i: (i, 0)),
                  pl.BlockSpec((1, D), lambda i: (0, 0))],
        out_specs=pl.BlockSpec((tm, D), lambda i: (i, 0)),
        out_shape=jax.ShapeDtypeStruct((T, D), BF16),
        compiler_params=_params("parallel"),
        name="rmsnorm",
    )(x, g.reshape(1, D))


def rmsnorm_small(x, g, w_small, aux, post, *, hn_dtype=BF16, tm=512, name="rmsnorm_small"):
    T, D = x.shape
    return pl.pallas_call(
        functools.partial(_norm_small_kernel, post=post),
        grid=(T // tm,),
        in_specs=[pl.BlockSpec((tm, D), lambda i: (i, 0)),
                  pl.BlockSpec((1, D), lambda i: (0, 0)),
                  pl.BlockSpec((D, LANES), lambda i: (0, 0)),
                  pl.BlockSpec((8, LANES), lambda i: (0, 0))],
        out_specs=[pl.BlockSpec((tm, D), lambda i: (i, 0)),
                   pl.BlockSpec((tm, LANES), lambda i: (i, 0))],
        out_shape=[jax.ShapeDtypeStruct((T, D), hn_dtype),
                   jax.ShapeDtypeStruct((T, LANES), F32)],
        compiler_params=_params("parallel"),
        name=name,
    )(x, g.reshape(1, D), w_small, aux)


def _pad_cols(w, n=LANES):
    return jnp.pad(w, ((0, 0), (0, n - w.shape[1])))


def _aux_rows(*rows):
    out = [jnp.pad(r.astype(F32), (0, LANES - r.shape[0])) for r in rows]
    out += [jnp.zeros((LANES,), F32)] * (8 - len(out))
    return jnp.stack(out)


def _post_fox(s, aux):
    return _log_sigmoid(s + aux[0:1, :])


def _post_gdn(s, aux, *, hv):
    lane = lax.broadcasted_iota(jnp.int32, s.shape, 1)
    beta = _sigmoid(s)
    g = -jnp.exp(aux[0:1, :]) * _softplus(s + aux[1:2, :])
    return jnp.where(lane < hv, beta, g)


def _post_moe(s, aux):
    del aux
    lane = lax.broadcasted_iota(jnp.int32, s.shape, 1).astype(F32)
    neg = jnp.float32(-jnp.inf)
    logits = jnp.where(lane < N_EXPERTS, s, neg)
    v1 = jnp.max(logits, axis=-1, keepdims=True)
    i1 = jnp.min(jnp.where(logits == v1, lane, float(LANES)), axis=-1, keepdims=True)
    rest = jnp.where(lane == i1, neg, logits)
    v2 = jnp.max(rest, axis=-1, keepdims=True)
    i2 = jnp.min(jnp.where(rest == v2, lane, float(LANES)), axis=-1, keepdims=True)
    e2 = jnp.exp(v2 - v1)
    denom = 1.0 + e2
    out = jnp.where(lane == 0.0, i1, 0.0)
    out = jnp.where(lane == 1.0, i2, out)
    out = jnp.where(lane == 2.0, 1.0 / denom, out)
    out = jnp.where(lane == 3.0, e2 / denom, out)
    return out


def _cumsum_kernel(x_ref, o_ref, *, blk):
    S = x_ref.shape[0]
    r = lax.broadcasted_iota(jnp.int32, (blk, blk), 0)
    c = lax.broadcasted_iota(jnp.int32, (blk, blk), 1)
    tri = (r >= c).astype(F32)
    carry = jnp.zeros((1, LANES), F32)
    for i in range(S // blk):
        cs = jnp.dot(tri, x_ref[i * blk:(i + 1) * blk, :], preferred_element_type=F32,
                     precision=HIGHEST) + carry
        o_ref[i * blk:(i + 1) * blk, :] = cs
        carry = cs[blk - 1:blk, :]


def seq_cumsum(x, batch, *, blk=128):
    T = x.shape[0]
    S = T // batch
    blk = min(blk, S)
    return pl.pallas_call(
        functools.partial(_cumsum_kernel, blk=blk),
        grid=(batch,),
        in_specs=[pl.BlockSpec((S, LANES), lambda b: (b, 0))],
        out_specs=pl.BlockSpec((S, LANES), lambda b: (b, 0)),
        out_shape=jax.ShapeDtypeStruct((T, LANES), F32),
        compiler_params=_params("parallel"),
        name="seq_cumsum",
    )(x)


def _chunk_cumsum_kernel(x_ref, o_ref, *, chunk, lo, hi):
    R = x_ref.shape[0]
    r = lax.broadcasted_iota(jnp.int32, (chunk, chunk), 0)
    c = lax.broadcasted_iota(jnp.int32, (chunk, chunk), 1)
    tri = (r >= c).astype(F32)
    lane = lax.broadcasted_iota(jnp.int32, (chunk, LANES), 1)
    sel = (lane >= lo) & (lane < hi)
    for i in range(R // chunk):
        x = x_ref[i * chunk:(i + 1) * chunk, :]
        cs = jnp.dot(tri, x, preferred_element_type=F32, precision=HIGHEST)
        o_ref[i * chunk:(i + 1) * chunk, :] = jnp.where(sel, cs, x)


def chunk_cumsum(x, *, chunk, lo, hi, tm=512):
    T = x.shape[0]
    tm = min(tm, T)
    return pl.pallas_call(
        functools.partial(_chunk_cumsum_kernel, chunk=chunk, lo=lo, hi=hi),
        grid=(T // tm,),
        in_specs=[pl.BlockSpec((tm, LANES), lambda i: (i, 0))],
        out_specs=pl.BlockSpec((tm, LANES), lambda i: (i, 0)),
        out_shape=jax.ShapeDtypeStruct((T, LANES), F32),
        compiler_params=_params("parallel"),
        name="chunk_cumsum",
    )(x)


def _mm_kernel(a_ref, w_ref, *rest, n_extra, epilogue):
    extra = rest[:n_extra]
    o_ref = rest[n_extra]
    scratch = rest[n_extra + 1:]
    wbf_ref = scratch[0]

    @pl.when(pl.program_id(1) == 0)
    def _():
        wbf_ref[...] = w_ref[...].astype(BF16)

    acc = _dot(a_ref[...], wbf_ref[...])
    epilogue(acc, extra, o_ref, scratch[1:])


def _layer_w_spec(K, tn, layer):
    return pl.BlockSpec((None, K, tn), lambda j, i: (layer, 0, j))


def matmul(a, w, layer, *, n_cols, tm, tn, out_dtype, epilogue, extra=(), extra_specs=(),
           extra_scratch=(), name="matmul"):
    M, K = a.shape
    grid = (n_cols // tn, M // tm)
    return pl.pallas_call(
        functools.partial(_mm_kernel, n_extra=len(extra), epilogue=epilogue),
        grid=grid,
        in_specs=[pl.BlockSpec((tm, K), lambda j, i: (i, 0)),
                  _layer_w_spec(K, tn, layer),
                  *extra_specs],
        out_specs=pl.BlockSpec((tm, tn), lambda j, i: (i, j)),
        out_shape=jax.ShapeDtypeStruct((M, n_cols), out_dtype),
        scratch_shapes=[pltpu.VMEM((K, tn), BF16), *extra_scratch],
        compiler_params=_params("arbitrary", "arbitrary"),
        name=name,
    )(a, w, *extra)


def _epi_residual(acc, extra, o_ref, scratch):
    del scratch
    o_ref[...] = extra[0][...] + acc


def matmul_residual(a, w, layer, res, *, tm, tn, name):
    N = w.shape[2]
    return matmul(a, w, layer, n_cols=N, tm=tm, tn=tn, out_dtype=F32, epilogue=_epi_residual,
                  extra=(res,), extra_specs=(pl.BlockSpec((tm, tn), lambda j, i: (i, j)),),
                  name=name)


def _epi_fox_in(acc, extra, o_ref, scratch, *, n_q_blocks):
    del scratch
    qg_ref, kg_ref = extra
    j = pl.program_id(0)
    tn = acc.shape[1]

    def normed(gain, scale):
        for h in range(tn // HEAD_DIM):
            blk = acc[:, h * HEAD_DIM:(h + 1) * HEAD_DIM]
            o_ref[:, h * HEAD_DIM:(h + 1) * HEAD_DIM] = (_rms(blk, gain) * scale).astype(o_ref.dtype)

    @pl.when(j < n_q_blocks)
    def _():
        normed(qg_ref[...], HEAD_DIM ** -0.5 * LOG2E)

    @pl.when((j >= n_q_blocks) & (j < 2 * n_q_blocks))
    def _():
        normed(kg_ref[...], 1.0)

    @pl.when(j >= 2 * n_q_blocks)
    def _():
        o_ref[...] = acc.astype(o_ref.dtype)


def _epi_gdn_in(acc, extra, o_ref, scratch, *, n_k_blocks, n_conv_blocks, rows_per_seq, q_scale):
    conv_ref, = extra
    buf_ref, = scratch
    j = pl.program_id(0)
    i = pl.program_id(1)
    tm, tn = acc.shape

    @pl.when(j < n_conv_blocks)
    def _():
        @pl.when(i % rows_per_seq == 0)
        def _():
            buf_ref[0:8, :] = jnp.zeros((8, tn), F32)

        buf_ref[8:8 + tm, :] = acc
        y = acc * conv_ref[GDN_CONV - 1:GDN_CONV, :]
        for tap in range(GDN_CONV - 1):
            shift = GDN_CONV - 1 - tap
            y = y + buf_ref[8 - shift:8 - shift + tm, :] * conv_ref[tap:tap + 1, :]
        buf_ref[0:8, :] = acc[tm - 8:tm, :]
        y = _silu(y)

        @pl.when(j < 2 * n_k_blocks)
        def _():
            scale = jnp.where(j < n_k_blocks, jnp.float32(q_scale), jnp.float32(1.0))
            for h in range(tn // HEAD_DIM):
                blk = y[:, h * HEAD_DIM:(h + 1) * HEAD_DIM]
                ss = jnp.sum(blk * blk, axis=-1, keepdims=True)
                o_ref[:, h * HEAD_DIM:(h + 1) * HEAD_DIM] = (
                    blk * lax.rsqrt(ss + EPS) * scale).astype(o_ref.dtype)

        @pl.when(j >= 2 * n_k_blocks)
        def _():
            o_ref[...] = y.astype(o_ref.dtype)

    @pl.when(j >= n_conv_blocks)
    def _():
        o_ref[...] = acc.astype(o_ref.dtype)


def _swiglu_up_kernel(a_ref, wg_ref, wu_ref, o_ref, wg_bf, wu_bf):
    @pl.when(pl.program_id(1) == 0)
    def _():
        wg_bf[...] = wg_ref[...].astype(BF16)
        wu_bf[...] = wu_ref[...].astype(BF16)

    a = a_ref[...]
    g = _dot(a, wg_bf[...])
    u = _dot(a, wu_bf[...])
    o_ref[...] = (_silu(g) * u).astype(o_ref.dtype)


def swiglu_up(a, wg, wu, layer, *, tm, tn):
    M, K = a.shape
    N = wg.shape[2]
    return pl.pallas_call(
        _swiglu_up_kernel,
        grid=(N // tn, M // tm),
        in_specs=[pl.BlockSpec((tm, K), lambda j, i: (i, 0)),
                  _layer_w_spec(K, tn, layer),
                  _layer_w_spec(K, tn, layer)],
        out_specs=pl.BlockSpec((tm, tn), lambda j, i: (i, j)),
        out_shape=jax.ShapeDtypeStruct((M, N), BF16),
        scratch_shapes=[pltpu.VMEM((K, tn), BF16), pltpu.VMEM((K, tn), BF16)],
        compiler_params=_params("arbitrary", "arbitrary"),
        name="swiglu_up",
    )(a, wg, wu)


def _fox_attn_kernel(q_ref, k_ref, v_ref, og_ref, cq_ref, ck_ref, o_ref,
                     s_scr, p_scr, m_ref, l_ref, alpha_ref, acc_ref, *, tq, nq):
    tk = tq
    h = pl.program_id(1)
    qi = pl.program_id(2)
    q = q_ref[...]
    lane = lax.broadcasted_iota(jnp.int32, cq_ref.shape, 1)
    cq = jnp.sum(jnp.where(lane == h, cq_ref[...], 0.0), axis=-1, keepdims=True) * LOG2E

    def ck(kb):
        return ck_ref[0, pl.ds(kb, 1), :] * LOG2E

    def scores(kb):
        return _dot_nt(q, k_ref[pl.ds(pl.multiple_of(kb * tk, tk), tk), :])

    def pv_dot(kb, slot):
        return _dot(p_scr[slot], v_ref[pl.ds(pl.multiple_of(kb * tk, tk), tk), :])

    def fold(pv):
        acc_ref[...] = alpha_ref[...] * acc_ref[...] + pv

    def softmax(t, slot, between):
        m_prev = m_ref[...]
        m_new = jnp.maximum(m_prev, jnp.max(t, axis=-1, keepdims=True) + cq)
        between()
        alpha = jnp.exp2(m_prev - m_new)
        p = jnp.exp2(t + (cq - m_new))
        l_ref[...] = alpha * l_ref[...] + jnp.sum(p, axis=-1, keepdims=True)
        alpha_ref[...] = alpha
        p_scr[slot] = p.astype(p_scr.dtype)
        m_ref[...] = m_new

    m_ref[...] = jnp.full(m_ref.shape, -jnp.inf, F32)
    l_ref[...] = jnp.zeros(l_ref.shape, F32)
    alpha_ref[...] = jnp.zeros(alpha_ref.shape, F32)
    acc_ref[...] = jnp.zeros(acc_ref.shape, F32)
    p_scr[1] = jnp.zeros(p_scr.shape[1:], p_scr.dtype)
    s_scr[0] = scores(0)

    def body(k, carry):
        cur = k % 2
        pv = pv_dot(jnp.maximum(k - 1, 0), 1 - cur)
        s_next = scores(k + 1)
        t = s_scr[cur] - ck(k)

        def between():
            fold(pv)
            s_scr[1 - cur] = s_next

        softmax(t, cur, between)
        return carry

    lax.fori_loop(0, qi, body, 0)

    cur = qi % 2
    pv = pv_dot(jnp.maximum(qi - 1, 0), 1 - cur)
    causal = (lax.broadcasted_iota(jnp.int32, (tq, tk), 0) >= lax.broadcasted_iota(jnp.int32, (tq, tk), 1))
    t = jnp.where(causal, s_scr[cur] - ck(qi), -jnp.inf)
    softmax(t, cur, lambda: fold(pv))
    fold(pv_dot(qi, cur))
    o = acc_ref[...] / l_ref[...]
    o_ref[...] = (o * _sigmoid(og_ref[...].astype(F32))).astype(o_ref.dtype)


def fox_attention(proj, cum, cum_rows, *, batch, heads, tq=512):
    T = proj.shape[0]
    S = T // batch
    nq = S // tq
    H = heads
    kern = functools.partial(_fox_attn_kernel, tq=tq, nq=nq)
    return pl.pallas_call(
        kern,
        grid=(batch, H, nq),
        in_specs=[
            pl.BlockSpec((tq, HEAD_DIM), lambda b, h, qi: (b * nq + qi, h)),
            pl.BlockSpec((S, HEAD_DIM), lambda b, h, qi: (b, H + h)),
            pl.BlockSpec((S, HEAD_DIM), lambda b, h, qi: (b, 2 * H + h)),
            pl.BlockSpec((tq, HEAD_DIM), lambda b, h, qi: (b * nq + qi, 3 * H + h)),
            pl.BlockSpec((tq, LANES), lambda b, h, qi: (b * nq + qi, 0)),
            pl.BlockSpec((1, nq, tq), lambda b, h, qi: (b * H + h, 0, 0)),
        ],
        out_specs=pl.BlockSpec((tq, HEAD_DIM), lambda b, h, qi: (b * nq + qi, h)),
        out_shape=jax.ShapeDtypeStruct((T, H * HEAD_DIM), BF16),
        scratch_shapes=[pltpu.VMEM((2, tq, tq), F32), pltpu.VMEM((2, tq, tq), BF16),
                        pltpu.VMEM((tq, 1), F32), pltpu.VMEM((tq, 1), F32), pltpu.VMEM((tq, 1), F32),
                        pltpu.VMEM((tq, HEAD_DIM), F32)],
        compiler_params=_params("parallel", "parallel", "arbitrary"),
        name="fox_attention",
    )(proj, proj, proj, proj, cum, cum_rows)


def _gdn_kernel(q_ref, k_ref, v_ref, z_ref, small_ref, grow_ref, gain_ref,
                o_ref, cols_ref, u_ref, w_ref, a_ref, o_scr, *, hv, nk, group):
    C = GDN_CHUNK
    S = q_ref.shape[0]
    n_chunks = S // C
    nv = 2 * nk
    hp = pl.program_id(1)
    cols = lambda i: slice(i * HEAD_DIM, (i + 1) * HEAD_DIM)

    cols_ref[...] = pltpu.roll(small_ref[...], (LANES - nv * hp) % LANES, 1)

    ri = lax.broadcasted_iota(jnp.int32, (C, C), 0)
    ci = lax.broadcasted_iota(jnp.int32, (C, C), 1)
    incl = ri >= ci
    strict = ri > ci
    eye = (ri == ci).astype(F32)
    steps = int(math.log2(C)) - 1

    def prep_stages(gi):
        chunk_rows = [pl.ds(pl.multiple_of((gi * group + t) * C, C), C) for t in range(group)]
        kf, qkk = {}, {}
        for t, rows in enumerate(chunk_rows):
            for kh in range(nk):
                kc = k_ref[rows, cols(kh)]
                qc = q_ref[rows, cols(kh)]
                kf[t, kh] = kc.astype(F32)
                qkk[t, kh] = _dot_nt(jnp.concatenate([kc, qc], axis=0), kc)
        yield
        chains = [(t, h) for t in range(group) for h in range(nv)]
        beta, g, x, lb = {}, {}, {}, {}
        for t, h in chains:
            rows = chunk_rows[t]
            g[t, h] = cols_ref[rows, hv + h:hv + h + 1]
            beta[t, h] = cols_ref[rows, h:h + 1]
            grow = grow_ref[h, pl.ds(gi * group + t, 1), :]
            decay = jnp.exp(jnp.where(incl, g[t, h] - grow, -jnp.inf))
            lmat = jnp.where(strict, beta[t, h] * qkk[t, h // 2][:C] * decay, 0.0)
            a_ref[h, rows, :] = jnp.where(incl, qkk[t, h // 2][C:] * decay, 0.0)
            x[t, h] = eye - lmat
            lb[t, h] = lmat.astype(BF16)
        p = {ch: _dot(lb[ch], lb[ch]) for ch in chains}
        yield
        for s in range(steps):
            if s < steps - 1:
                px = {ch: _dot(jnp.concatenate([p[ch], x[ch]], axis=0).astype(BF16), p[ch].astype(BF16))
                      for ch in chains}
                for ch in chains:
                    p[ch], x[ch] = px[ch][:C], x[ch] + px[ch][C:]
            else:
                xp = {ch: _dot(x[ch].astype(BF16), p[ch].astype(BF16)) for ch in chains}
                for ch in chains:
                    x[ch] = x[ch] + xp[ch]
            yield
        sol = {}
        for t, h in chains:
            vf = v_ref[chunk_rows[t], cols(h)].astype(F32)
            rhs = jnp.concatenate([vf * beta[t, h], kf[t, h // 2] * (beta[t, h] * jnp.exp(g[t, h]))], axis=1)
            sol[t, h] = _dot(x[t, h].astype(BF16), rhs.astype(BF16))
        for t, h in chains:
            u_ref[h, chunk_rows[t], :] = sol[t, h][:, :HEAD_DIM]
            w_ref[h, chunk_rows[t], :] = sol[t, h][:, HEAD_DIM:]
        yield

    def rec_stages(gi, states):
        for t in range(group):
            c = gi * group + t
            rows = pl.ds(pl.multiple_of(c * C, C), C)
            last = pl.ds(c * C + C - 1, 1)
            kf = [k_ref[rows, cols(kh)].astype(F32) for kh in range(nk)]
            qf = [q_ref[rows, cols(kh)].astype(F32) for kh in range(nk)]
            g = [cols_ref[rows, hv + h:hv + h + 1] for h in range(nv)]
            g_last = [cols_ref[last, hv + h:hv + h + 1] for h in range(nv)]
            ws_qs = []
            for h in range(nv):
                wq = jnp.concatenate([w_ref[h, rows, :], qf[h // 2] * jnp.exp(g[h])], axis=0)
                ws_qs.append(_dot(wq.astype(BF16), states[h].astype(BF16)))
            yield
            vb = [(u_ref[h, rows, :] - ws_qs[h][:C]).astype(BF16) for h in range(nv)]
            av = [_dot(a_ref[h, rows, :].astype(BF16), vb[h]) for h in range(nv)]
            kv = [_dot_tn((kf[h // 2] * jnp.exp(g_last[h] - g[h])).astype(BF16), vb[h]) for h in range(nv)]
            for h in range(nv):
                o_scr[h, rows, :] = ws_qs[h][C:] + av[h]
                states[h] = states[h] * jnp.exp(g_last[h]) + kv[h]
            yield

    def run_interleaved(*gens):
        live = list(gens)
        while live:
            for gen in list(live):
                if next(gen, "done") == "done":
                    live.remove(gen)

    n_groups = n_chunks // group
    run_interleaved(prep_stages(0))

    def body(gi, states):
        states = list(states)
        run_interleaved(rec_stages(gi, states), prep_stages(gi + 1))
        return tuple(states)

    zero = jnp.zeros((HEAD_DIM, HEAD_DIM), F32)
    states = list(lax.fori_loop(0, n_groups - 1, body, (zero,) * nv))
    run_interleaved(rec_stages(n_groups - 1, states))

    for h in range(nv):
        o = _rms(o_scr[h], gain_ref[...])
        z = z_ref[:, cols(h)].astype(F32)
        o_ref[:, cols(h)] = (o * _silu(z)).astype(o_ref.dtype)


def gated_delta(proj, small, grow, o_gain, *, batch, k_heads, nk=2, group=4):
    T = proj.shape[0]
    S = T // batch
    HK = k_heads
    HV = 2 * HK
    C = GDN_CHUNK
    n_chunks = S // C
    nk = min(nk, HK)
    nv = 2 * nk
    NP = HK // nk
    group = min(group, n_chunks)
    kern = functools.partial(_gdn_kernel, hv=HV, nk=nk, group=group)
    kw = nk * HEAD_DIM
    vw = nv * HEAD_DIM
    return pl.pallas_call(
        kern,
        grid=(batch, NP),
        in_specs=[
            pl.BlockSpec((S, kw), lambda b, h: (b, h)),
            pl.BlockSpec((S, kw), lambda b, h: (b, NP + h)),
            pl.BlockSpec((S, vw), lambda b, h: (b, NP + h)),
            pl.BlockSpec((S, vw), lambda b, h: (b, 2 * NP + h)),
            pl.BlockSpec((S, LANES), lambda b, h: (b, 0)),
            pl.BlockSpec((nv, n_chunks, C), lambda b, h: (b * NP + h, 0, 0)),
            pl.BlockSpec((1, HEAD_DIM), lambda b, h: (0, 0)),
        ],
        out_specs=pl.BlockSpec((S, vw), lambda b, h: (b, h)),
        out_shape=jax.ShapeDtypeStruct((T, HV * HEAD_DIM), BF16),
        scratch_shapes=[pltpu.VMEM((S, LANES), F32),
                        pltpu.VMEM((nv, S, HEAD_DIM), F32), pltpu.VMEM((nv, S, HEAD_DIM), F32),
                        pltpu.VMEM((nv, S, C), F32), pltpu.VMEM((nv, S, HEAD_DIM), F32)],
        compiler_params=_params("parallel", "parallel"),
        name="gated_delta",
    )(proj, proj, proj, proj, small, grow, o_gain.reshape(1, HEAD_DIM))


def _row_copy(src_ref, dst_ref, src_row, dst_row, sem):
    return pltpu.make_async_copy(src_ref.at[pl.ds(src_row, 1), :], dst_ref.at[pl.ds(dst_row, 1), :], sem)


def _gather_rows_kernel(idx_ref, src_ref, o_ref, buf_ref, sem, *, rows):
    t = pl.program_id(0)
    slot = t % 2

    def issue(step, sl):
        base = step * rows

        def start(r, c):
            _row_copy(src_ref, buf_ref.at[sl], idx_ref[base + r], r, sem.at[sl]).start()
            return c

        lax.fori_loop(0, rows, start, 0, unroll=8)

    @pl.when(t == 0)
    def _():
        issue(0, 0)

    @pl.when(t + 1 < pl.num_programs(0))
    def _():
        issue(t + 1, 1 - slot)

    def wait(r, c):
        _row_copy(src_ref, buf_ref.at[slot], 0, r, sem.at[slot]).wait()
        return c

    lax.fori_loop(0, rows, wait, 0, unroll=8)
    o_ref[...] = buf_ref[slot].astype(o_ref.dtype)


def gather_rows(src, idx, *, rows=256):
    P = idx.shape[0]
    D = src.shape[1]
    return pl.pallas_call(
        functools.partial(_gather_rows_kernel, rows=rows),
        grid_spec=pltpu.PrefetchScalarGridSpec(
            num_scalar_prefetch=1,
            grid=(P // rows,),
            in_specs=[pl.BlockSpec(memory_space=pl.ANY)],
            out_specs=pl.BlockSpec((rows, D), lambda t, idx: (t, 0)),
            scratch_shapes=[pltpu.VMEM((2, rows, D), F32), pltpu.SemaphoreType.DMA((2,))]),
        out_shape=jax.ShapeDtypeStruct((P, D), BF16),
        compiler_params=_params("arbitrary"),
        name="moe_gather",
    )(idx, src)


def _combine_kernel(pos_ref, y_ref, h_ref, small_ref, o_ref, buf_ref, sem, *, rows):
    t = pl.program_id(0)
    slot = t % 2

    def issue(step, sl):
        base = step * rows

        def start(r, c):
            for k in range(2):
                _row_copy(y_ref, buf_ref.at[sl, k], pos_ref[k, base + r], r, sem.at[sl]).start()
            return c

        lax.fori_loop(0, rows, start, 0, unroll=4)

    @pl.when(t == 0)
    def _():
        issue(0, 0)

    @pl.when(t + 1 < pl.num_programs(0))
    def _():
        issue(t + 1, 1 - slot)

    def wait(r, c):
        for k in range(2):
            _row_copy(y_ref, buf_ref.at[slot, k], 0, r, sem.at[slot]).wait()
        return c

    lax.fori_loop(0, rows, wait, 0, unroll=4)
    w1 = small_ref[:, 2:3]
    w2 = small_ref[:, 3:4]
    o_ref[...] = h_ref[...] + (w1 * buf_ref[slot, 0] + w2 * buf_ref[slot, 1])


def moe_combine(y, pos, h, small, *, rows=256):
    T, D = h.shape
    return pl.pallas_call(
        functools.partial(_combine_kernel, rows=rows),
        grid_spec=pltpu.PrefetchScalarGridSpec(
            num_scalar_prefetch=1,
            grid=(T // rows,),
            in_specs=[pl.BlockSpec(memory_space=pl.ANY),
                      pl.BlockSpec((rows, D), lambda t, pos: (t, 0)),
                      pl.BlockSpec((rows, LANES), lambda t, pos: (t, 0))],
            out_specs=pl.BlockSpec((rows, D), lambda t, pos: (t, 0)),
            scratch_shapes=[pltpu.VMEM((2, 2, rows, D), F32), pltpu.SemaphoreType.DMA((2,))]),
        out_shape=jax.ShapeDtypeStruct((T, D), F32),
        compiler_params=_params("arbitrary"),
        name="moe_combine",
    )(pos, y, h, small)


def _grouped_kernel(te_ref, nv_ref, nxt_ref, x_ref, *rest, n_w, layer, tn, finish):
    w_hbm = rest[:n_w]
    o_ref = rest[n_w]
    stage = rest[n_w + 1:2 * n_w + 1]
    wbf = rest[2 * n_w + 1:3 * n_w + 1]
    sem = rest[3 * n_w + 1]
    j = pl.program_id(0)
    t = pl.program_id(1)
    e = te_ref[t]
    first = (t == 0) | (e != te_ref[jnp.maximum(t - 1, 0)])
    valid = t < nv_ref[0]

    def copies(jj, ee):
        cols = pl.ds(pl.multiple_of(jj * tn, LANES), tn)
        return [pltpu.make_async_copy(w_hbm[i].at[layer, ee, :, cols], stage[i], sem.at[i])
                for i in range(n_w)]

    @pl.when((j == 0) & (t == 0))
    def _():
        for c in copies(j, e):
            c.start()

    @pl.when(valid & first)
    def _():
        for i, c in enumerate(copies(j, e)):
            c.wait()
            wbf[i][...] = stage[i][...].astype(BF16)
        nt = nxt_ref[t]

        @pl.when(nt >= 0)
        def _():
            for c in copies(j, te_ref[jnp.maximum(nt, 0)]):
                c.start()

        @pl.when((nt < 0) & (j + 1 < pl.num_programs(0)))
        def _():
            for c in copies(j + 1, te_ref[0]):
                c.start()

    @pl.when(valid)
    def _():
        x = x_ref[...]
        o_ref[...] = finish([_dot(x, w[...]) for w in wbf]).astype(o_ref.dtype)

    @pl.when(jnp.logical_not(valid))
    def _():
        o_ref[...] = jnp.zeros(o_ref.shape, o_ref.dtype)


def grouped_matmul(x, weights, layer, tile_expert, n_valid, next_tile, *, tm, tn, out_dtype, finish, name):
    P, K = x.shape
    N = weights[0].shape[3]
    n_w = len(weights)
    last = lambda t, nv: jnp.minimum(t, nv[0] - 1)
    return pl.pallas_call(
        functools.partial(_grouped_kernel, n_w=n_w, layer=layer, tn=tn, finish=finish),
        grid_spec=pltpu.PrefetchScalarGridSpec(
            num_scalar_prefetch=3,
            grid=(N // tn, P // tm),
            in_specs=[pl.BlockSpec((tm, K), lambda j, t, te, nv, nx: (last(t, nv), 0)),
                      *[pl.BlockSpec(memory_space=pl.ANY)] * n_w],
            out_specs=pl.BlockSpec((tm, tn), lambda j, t, te, nv, nx: (t, j)),
            scratch_shapes=[*[pltpu.VMEM((K, tn), F32)] * n_w, *[pltpu.VMEM((K, tn), BF16)] * n_w,
                            pltpu.SemaphoreType.DMA((n_w,))]),
        out_shape=jax.ShapeDtypeStruct((P, N), out_dtype),
        compiler_params=_params("arbitrary", "arbitrary"),
        name=name,
    )(tile_expert, n_valid, next_tile, x, *weights)


def _moe_plan(small, *, tm):
    T = small.shape[0]
    experts = small[:, 0:2].astype(jnp.int32)
    flat = experts.T.reshape(-1)
    onehot = (flat[:, None] == jnp.arange(N_EXPERTS)[None, :]).astype(jnp.int32)
    rank = jnp.take_along_axis(jnp.cumsum(onehot, axis=0) - onehot, flat[:, None], axis=1)[:, 0]
    counts = jnp.sum(onehot, axis=0)
    tiles_per = (counts + tm - 1) // tm
    tile_end = jnp.cumsum(tiles_per)
    tile_start = tile_end - tiles_per
    pos = tile_start[flat] * tm + rank
    n_tiles = (2 * T) // tm + N_EXPERTS
    n_valid = tile_end[-1]
    tile_ids = jnp.minimum(jnp.arange(n_tiles), n_valid - 1)
    tile_expert = jnp.sum((tile_ids[:, None] >= tile_end[None, :]).astype(jnp.int32), axis=1)
    token = jnp.tile(jnp.arange(T, dtype=jnp.int32), 2)
    row_token = jnp.zeros((n_tiles * tm,), jnp.int32).at[pos].set(token)
    next_tile = jnp.where(tile_end[tile_expert] < n_valid, tile_end[tile_expert], -1)
    return (pos.reshape(2, T).astype(jnp.int32), row_token, tile_expert.astype(jnp.int32),
            n_valid.reshape(1).astype(jnp.int32), next_tile.astype(jnp.int32))


def moe_ffn(h, hn, small, wg, wu, wd, layer, *, tm=512):
    tm = min(tm, h.shape[0])
    pos, row_token, tile_expert, n_valid, next_tile = _moe_plan(small, tm=tm)
    xs = gather_rows(hn, row_token, rows=min(256, tm))
    N = wg.shape[3]
    tn_up = N // 2 if (N // 2) % LANES == 0 else N
    mid = grouped_matmul(xs, (wg, wu), layer, tile_expert, n_valid, next_tile, tm=tm, tn=tn_up,
                         out_dtype=BF16, finish=lambda d: _silu(d[0]) * d[1], name="expert_up")
    y = grouped_matmul(mid, (wd,), layer, tile_expert, n_valid, next_tile, tm=tm,
                       tn=min(1024, wd.shape[3]), out_dtype=F32, finish=lambda d: d[0], name="expert_down")
    return moe_combine(y, pos, h, small, rows=min(256, tm))


def _fox_layer(h, batch, j, norm_mix, norm_ffn, w_in, b_f, q_gain, k_gain, w_out, w_gate, w_up,
               w_down):
    T, D = h.shape
    S = T // batch
    H = b_f.shape[0]
    width = H * HEAD_DIM
    hn, lf = rmsnorm_small(h, norm_mix, _pad_cols(w_in[j, :, 4 * width:]), _aux_rows(b_f), _post_fox,
                           name="fox_norm")
    cum = seq_cumsum(lf, batch)
    tq = min(512, S)
    cum_rows = cum[:, :H].reshape(batch, S, H).transpose(0, 2, 1).reshape(batch * H, S // tq, tq)
    tn = min(1024, width)
    proj = matmul(hn, w_in, j, n_cols=4 * width, tm=min(1024, T), tn=tn, out_dtype=BF16,
                  epilogue=functools.partial(_epi_fox_in, n_q_blocks=width // tn),
                  extra=(q_gain.reshape(1, HEAD_DIM), k_gain.reshape(1, HEAD_DIM)),
                  extra_specs=(pl.BlockSpec((1, HEAD_DIM), lambda j, i: (0, 0)),) * 2,
                  name="fox_in_proj")
    o = fox_attention(proj, cum, cum_rows, batch=batch, heads=H, tq=tq)
    h = matmul_residual(o, w_out, j, h, tm=min(1024, T), tn=min(1024, D), name="fox_out_proj")
    hn2 = rmsnorm(h, norm_ffn)
    F = w_gate.shape[2]
    mid = swiglu_up(hn2, w_gate, w_up, j, tm=min(1024, T), tn=512 if F % 512 == 0 else F)
    return matmul_residual(mid, w_down, j, h, tm=min(512, T), tn=min(512, D), name="ffn_down")


def _gdn_layer(h, batch, j, norm_mix, norm_ffn, w_in, conv_w, a_log, dt_bias, o_gain, w_out,
               router, wg, wu, wd, *, seq_tile=1024):
    T, D = h.shape
    S = T // batch
    HV = a_log.shape[0]
    HK = HV // 2
    kw = HK * HEAD_DIM
    conv_ch = 4 * kw
    main = conv_ch + 2 * kw
    zeros = jnp.zeros((HV,), F32)
    aux = _aux_rows(jnp.concatenate([zeros, a_log]), jnp.concatenate([zeros, dt_bias]))
    hn, small = rmsnorm_small(h, norm_mix, _pad_cols(w_in[j, :, main:]), aux,
                              functools.partial(_post_gdn, hv=HV), name="gdn_norm")
    small = chunk_cumsum(small, chunk=min(GDN_CHUNK, S), lo=HV, hi=2 * HV)
    C = GDN_CHUNK
    grow = small[:, HV:2 * HV].reshape(batch, S, HV).transpose(0, 2, 1).reshape(batch * HV, S // C, C)
    tn = min(1024, kw)
    tm = min(seq_tile, S)
    proj = matmul(hn, w_in, j, n_cols=main, tm=tm, tn=tn, out_dtype=BF16,
                  epilogue=functools.partial(_epi_gdn_in, n_k_blocks=kw // tn,
                                             n_conv_blocks=conv_ch // tn, rows_per_seq=S // tm,
                                             q_scale=HEAD_DIM ** -0.5),
                  extra=(conv_w,),
                  extra_specs=(pl.BlockSpec((GDN_CONV, tn),
                                            lambda j, i: (0, jnp.minimum(j, conv_ch // tn - 1))),),
                  extra_scratch=(pltpu.VMEM((8 + tm, tn), F32),),
                  name="gdn_in_proj")
    o = gated_delta(proj, small, grow, o_gain, batch=batch, k_heads=HK)
    h = matmul_residual(o, w_out, j, h, tm=min(1024, T), tn=min(512, D), name="gdn_out_proj")
    hn2, route = rmsnorm_small(h, norm_ffn, _pad_cols(router), _aux_rows(), _post_moe,
                               hn_dtype=F32, name="moe_norm")
    return moe_ffn(h, hn2, route, wg, wu, wd, j)


def kernel(x, norm_mix, norm_ffn, fox_w_in, fox_b_f, fox_q_norm, fox_k_norm, fox_w_out,
           gdn_w_in, gdn_conv, gdn_a_log, gdn_dt_bias, gdn_o_norm, gdn_w_out,
           ffn_w_gate, ffn_w_up, ffn_w_down,
           moe_router, moe_w_gate, moe_w_up, moe_w_down):
    B, S, D = x.shape
    h = x.reshape(B * S, D)
    depth = norm_mix.shape[0]
    for i in range(depth):
        j = i // 2
        if i % 2 == 0:
            h = _fox_layer(h, B, j, norm_mix[i], norm_ffn[i], fox_w_in, fox_b_f[j], fox_q_norm[j],
                           fox_k_norm[j], fox_w_out, ffn_w_gate, ffn_w_up, ffn_w_down)
        else:
            h = _gdn_layer(h, B, j, norm_mix[i], norm_ffn[i], gdn_w_in, gdn_conv[j], gdn_a_log[j],
                           gdn_dt_bias[j], gdn_o_norm[j], gdn_w_out, moe_router[j],
                           moe_w_gate, moe_w_up, moe_w_down)
    return h.reshape(B, S, D)
```

```python
import functools
import math

import jax
import jax.numpy as jnp
from jax import lax
from jax.experimental import pallas as pl
from jax.experimental.pallas import tpu as pltpu

F32 = jnp.float32
BF16 = jnp.bfloat16
EPS = 1e-6
LANES = 128
HEAD_DIM = 128
GDN_CHUNK = 64
GDN_CONV = 4
N_EXPERTS = 8
VMEM_LIMIT_BYTES = 56 * 1024 * 1024
HIGHEST = lax.Precision.HIGHEST
LOG2E = math.log2(math.e)
DMA_ISSUE_UNROLL = 8


def _params(*sem):
    return pltpu.CompilerParams(dimension_semantics=sem, vmem_limit_bytes=VMEM_LIMIT_BYTES)


def _log1pexp_neg_abs(x):
    return jnp.log(1.0 + jnp.exp(-jnp.abs(x)))


def _log_sigmoid(x):
    return jnp.minimum(x, 0.0) - _log1pexp_neg_abs(x)


def _softplus(x):
    return jnp.maximum(x, 0.0) + _log1pexp_neg_abs(x)


def _sigmoid(x):
    return 1.0 / (1.0 + jnp.exp(-x))


def _silu(x):
    return x * _sigmoid(x)


def _dot(a, b):
    return jnp.dot(a, b, preferred_element_type=F32)


def _dot_nt(a, b):
    return lax.dot_general(a, b, (((1,), (1,)), ((), ())), preferred_element_type=F32)


def _dot_tn(a, b):
    return lax.dot_general(a, b, (((0,), (0,)), ((), ())), preferred_element_type=F32)


def _rms(x, g):
    ms = jnp.mean(x * x, axis=-1, keepdims=True)
    return x * lax.rsqrt(ms + EPS) * g


def _norm_kernel(x_ref, g_ref, hn_ref):
    hn_ref[...] = _rms(x_ref[...], g_ref[...]).astype(hn_ref.dtype)


def _norm_small_kernel(x_ref, g_ref, ws_ref, aux_ref, hn_ref, small_ref, *, post):
    y = _rms(x_ref[...], g_ref[...])
    hn_ref[...] = y.astype(hn_ref.dtype)
    s = jnp.dot(y, ws_ref[...], preferred_element_type=F32, precision=HIGHEST)
    small_ref[...] = post(s, aux_ref[...])


def rmsnorm(x, g, *, tm=512):
    T, D = x.shape
    return pl.pallas_call(
        _norm_kernel,
        grid=(T // tm,),
        in_specs=[pl.BlockSpec((tm, D), lambda i: (i, 0)),
                  pl.BlockSpec((1, D), lambda i: (0, 0))],
        out_specs=pl.BlockSpec((tm, D), lambda i: (i, 0)),
        out_shape=jax.ShapeDtypeStruct((T, D), BF16),
        compiler_params=_params("parallel"),
        name="rmsnorm",
    )(x, g.reshape(1, D))


def rmsnorm_small(x, g, w_small, aux, post, *, hn_dtype=BF16, tm=512, name="rmsnorm_small"):
    T, D = x.shape
    return pl.pallas_call(
        functools.partial(_norm_small_kernel, post=post),
        grid=(T // tm,),
        in_specs=[pl.BlockSpec((tm, D), lambda i: (i, 0)),
                  pl.BlockSpec((1, D), lambda i: (0, 0)),
                  pl.BlockSpec((D, LANES), lambda i: (0, 0)),
                  pl.BlockSpec((8, LANES), lambda i: (0, 0))],
        out_specs=[pl.BlockSpec((tm, D), lambda i: (i, 0)),
                   pl.BlockSpec((tm, LANES), lambda i: (i, 0))],
        out_shape=[jax.ShapeDtypeStruct((T, D), hn_dtype),
                   jax.ShapeDtypeStruct((T, LANES), F32)],
        compiler_params=_params("parallel"),
        name=name,
    )(x, g.reshape(1, D), w_small, aux)


def _pad_cols(w, n=LANES):
    return jnp.pad(w, ((0, 0), (0, n - w.shape[1])))


def _aux_rows(*rows):
    out = [jnp.pad(r.astype(F32), (0, LANES - r.shape[0])) for r in rows]
    out += [jnp.zeros((LANES,), F32)] * (8 - len(out))
    return jnp.stack(out)


def _post_fox(s, aux):
    return _log_sigmoid(s + aux[0:1, :])


def _post_gdn(s, aux, *, hv):
    lane = lax.broadcasted_iota(jnp.int32, s.shape, 1)
    beta = _sigmoid(s)
    g = -jnp.exp(aux[0:1, :]) * _softplus(s + aux[1:2, :])
    return jnp.where(lane < hv, beta, g)


def _post_moe(s, aux):
    del aux
    lane = lax.broadcasted_iota(jnp.int32, s.shape, 1).astype(F32)
    neg = jnp.float32(-jnp.inf)
    logits = jnp.where(lane < N_EXPERTS, s, neg)
    v1 = jnp.max(logits, axis=-1, keepdims=True)
    i1 = jnp.min(jnp.where(logits == v1, lane, float(LANES)), axis=-1, keepdims=True)
    rest = jnp.where(lane == i1, neg, logits)
    v2 = jnp.max(rest, axis=-1, keepdims=True)
    i2 = jnp.min(jnp.where(rest == v2, lane, float(LANES)), axis=-1, keepdims=True)
    e2 = jnp.exp(v2 - v1)
    denom = 1.0 + e2
    out = jnp.where(lane == 0.0, i1, 0.0)
    out = jnp.where(lane == 1.0, i2, out)
    out = jnp.where(lane == 2.0, 1.0 / denom, out)
    out = jnp.where(lane == 3.0, e2 / denom, out)
    return out


def _cumsum_kernel(x_ref, o_ref, *, blk):
    S = x_ref.shape[0]
    r = lax.broadcasted_iota(jnp.int32, (blk, blk), 0)
    c = lax.broadcasted_iota(jnp.int32, (blk, blk), 1)
    tri = (r >= c).astype(F32)
    carry = jnp.zeros((1, LANES), F32)
    for i in range(S // blk):
        cs = jnp.dot(tri, x_ref[i * blk:(i + 1) * blk, :], preferred_element_type=F32,
                     precision=HIGHEST) + carry
        o_ref[i * blk:(i + 1) * blk, :] = cs
        carry = cs[blk - 1:blk, :]


def seq_cumsum(x, batch, *, blk=128):
    T = x.shape[0]
    S = T // batch
    blk = min(blk, S)
    return pl.pallas_call(
        functools.partial(_cumsum_kernel, blk=blk),
        grid=(batch,),
        in_specs=[pl.BlockSpec((S, LANES), lambda b: (b, 0))],
        out_specs=pl.BlockSpec((S, LANES), lambda b: (b, 0)),
        out_shape=jax.ShapeDtypeStruct((T, LANES), F32),
        compiler_params=_params("parallel"),
        name="seq_cumsum",
    )(x)


def _chunk_cumsum_kernel(x_ref, o_ref, *, chunk, lo, hi):
    R = x_ref.shape[0]
    r = lax.broadcasted_iota(jnp.int32, (chunk, chunk), 0)
    c = lax.broadcasted_iota(jnp.int32, (chunk, chunk), 1)
    tri = (r >= c).astype(F32)
    lane = lax.broadcasted_iota(jnp.int32, (chunk, LANES), 1)
    sel = (lane >= lo) & (lane < hi)
    for i in range(R // chunk):
        x = x_ref[i * chunk:(i + 1) * chunk, :]
        cs = jnp.dot(tri, x, preferred_element_type=F32, precision=HIGHEST)
        o_ref[i * chunk:(i + 1) * chunk, :] = jnp.where(sel, cs, x)


def chunk_cumsum(x, *, chunk, lo, hi, tm=512):
    T = x.shape[0]
    tm = min(tm, T)
    return pl.pallas_call(
        functools.partial(_chunk_cumsum_kernel, chunk=chunk, lo=lo, hi=hi),
        grid=(T // tm,),
        in_specs=[pl.BlockSpec((tm, LANES), lambda i: (i, 0))],
        out_specs=pl.BlockSpec((tm, LANES), lambda i: (i, 0)),
        out_shape=jax.ShapeDtypeStruct((T, LANES), F32),
        compiler_params=_params("parallel"),
        name="chunk_cumsum",
    )(x)


def _mm_kernel(a_ref, w_ref, *rest, n_extra, variants, n_sub, w_transposed):
    extra = rest[:n_extra]
    o_ref = rest[n_extra]
    scratch = rest[n_extra + 1:]
    wbf_ref = scratch[0]

    @pl.when(pl.program_id(1) == 0)
    def _():
        w = w_ref[...]
        wbf_ref[...] = (w.T if w_transposed else w).astype(BF16)

    rs = a_ref.shape[0] // n_sub
    for cond, fn in variants(pl.program_id(0), pl.program_id(1), extra, o_ref, scratch[1:]):
        @pl.when(cond)
        def _(fn=fn):
            pending = None
            for sub in range(n_sub):
                rows = slice(sub * rs, (sub + 1) * rs)
                acc = _dot(a_ref[rows, :], wbf_ref[...])
                if pending is not None:
                    fn(*pending)
                pending = (acc, rows)
            fn(*pending)


def _layer_w_spec(K, tn, layer):
    return pl.BlockSpec((None, K, tn), lambda j, i: (layer, 0, j))


def matmul(a, w, layer, *, n_cols, tm, tn, out_dtype, variants, extra=(), extra_specs=(),
           extra_scratch=(), n_sub=4, w_transposed=False, name="matmul"):
    M, K = a.shape
    grid = (n_cols // tn, M // tm)
    w_spec = (pl.BlockSpec((None, tn, K), lambda j, i: (layer, j, 0)) if w_transposed
              else _layer_w_spec(K, tn, layer))
    return pl.pallas_call(
        functools.partial(_mm_kernel, n_extra=len(extra), variants=variants, n_sub=n_sub,
                          w_transposed=w_transposed),
        grid=grid,
        in_specs=[pl.BlockSpec((tm, K), lambda j, i: (i, 0)),
                  w_spec,
                  *extra_specs],
        out_specs=pl.BlockSpec((tm, tn), lambda j, i: (i, j)),
        out_shape=jax.ShapeDtypeStruct((M, n_cols), out_dtype),
        scratch_shapes=[pltpu.VMEM((K, tn), BF16), *extra_scratch],
        compiler_params=_params("arbitrary", "arbitrary"),
        name=name,
    )(a, w, *extra)


def _var_residual(j, i, extra, o_ref, scratch):
    del j, i, scratch

    def fn(acc, rows):
        o_ref[rows, :] = extra[0][rows, :] + acc

    return [(True, fn)]


def matmul_residual(a, w, layer, res, *, tm, tn, name):
    N = w.shape[2]
    return matmul(a, w, layer, n_cols=N, tm=tm, tn=tn, out_dtype=F32, variants=_var_residual,
                  extra=(res,), extra_specs=(pl.BlockSpec((tm, tn), lambda j, i: (i, j)),),
                  name=name)


def _store_cast(o_ref):
    def fn(acc, rows):
        o_ref[rows, :] = acc.astype(o_ref.dtype)
    return fn


def _var_fox_in(j, i, extra, o_ref, scratch, *, n_q_blocks):
    del i, scratch
    qg_ref, kg_ref = extra
    tn = o_ref.shape[1]

    def normed(gain_ref, scale):
        def fn(acc, rows):
            for h in range(tn // HEAD_DIM):
                cols = slice(h * HEAD_DIM, (h + 1) * HEAD_DIM)
                o_ref[rows, cols] = (_rms(acc[:, cols], gain_ref[...]) * scale).astype(o_ref.dtype)
        return fn

    return [(j < n_q_blocks, normed(qg_ref, HEAD_DIM ** -0.5 * LOG2E)),
            ((j >= n_q_blocks) & (j < 2 * n_q_blocks), normed(kg_ref, 1.0)),
            (j >= 2 * n_q_blocks, _store_cast(o_ref))]


def _var_gdn_in(j, i, extra, o_ref, scratch, *, n_k_blocks, n_conv_blocks, rows_per_seq, q_scale):
    conv_ref, = extra
    buf_ref, = scratch
    tm, tn = o_ref.shape

    @pl.when(i % rows_per_seq == 0)
    def _():
        buf_ref[0:8, :] = jnp.zeros((8, tn), F32)

    def conv(l2norm):
        def fn(acc, rows):
            r0, r1 = rows.start, rows.stop
            buf_ref[8 + r0:8 + r1, :] = acc
            y = acc * conv_ref[GDN_CONV - 1:GDN_CONV, :]
            for tap in range(GDN_CONV - 1):
                shift = GDN_CONV - 1 - tap
                y = y + buf_ref[8 + r0 - shift:8 + r1 - shift, :] * conv_ref[tap:tap + 1, :]
            if r1 == tm:
                buf_ref[0:8, :] = acc[r1 - r0 - 8:, :]
            y = _silu(y)
            if l2norm:
                scale = jnp.where(j < n_k_blocks, jnp.float32(q_scale), jnp.float32(1.0))
                for h in range(tn // HEAD_DIM):
                    cols = slice(h * HEAD_DIM, (h + 1) * HEAD_DIM)
                    blk = y[:, cols]
                    ss = jnp.sum(blk * blk, axis=-1, keepdims=True)
                    o_ref[rows, cols] = (blk * lax.rsqrt(ss + EPS) * scale).astype(o_ref.dtype)
            else:
                o_ref[rows, :] = y.astype(o_ref.dtype)
        return fn

    return [(j < 2 * n_k_blocks, conv(True)),
            ((j >= 2 * n_k_blocks) & (j < n_conv_blocks), conv(False)),
            (j >= n_conv_blocks, _store_cast(o_ref))]


def _swiglu_up_kernel(a_ref, wg_ref, wu_ref, o_ref, wg_bf, wu_bf):
    @pl.when(pl.program_id(1) == 0)
    def _():
        wg_bf[...] = wg_ref[...].astype(BF16)
        wu_bf[...] = wu_ref[...].astype(BF16)

    a = a_ref[...]
    g = _dot(a, wg_bf[...])
    u = _dot(a, wu_bf[...])
    o_ref[...] = (_silu(g) * u).astype(o_ref.dtype)


def swiglu_up(a, wg, wu, layer, *, tm, tn):
    M, K = a.shape
    N = wg.shape[2]
    return pl.pallas_call(
        _swiglu_up_kernel,
        grid=(N // tn, M // tm),
        in_specs=[pl.BlockSpec((tm, K), lambda j, i: (i, 0)),
                  _layer_w_spec(K, tn, layer),
                  _layer_w_spec(K, tn, layer)],
        out_specs=pl.BlockSpec((tm, tn), lambda j, i: (i, j)),
        out_shape=jax.ShapeDtypeStruct((M, N), BF16),
        scratch_shapes=[pltpu.VMEM((K, tn), BF16), pltpu.VMEM((K, tn), BF16)],
        compiler_params=_params("arbitrary", "arbitrary"),
        name="swiglu_up",
    )(a, wg, wu)


def _fox_attn_kernel(q_ref, k_ref, v_ref, og_ref, cq_ref, ck_ref, o_ref,
                     s_scr, p_scr, m_ref, l_ref, alpha_ref, acc_ref, *, tq, nq):
    tk = tq
    h = pl.program_id(1)
    qi = pl.program_id(2)
    q = q_ref[...]
    lane = lax.broadcasted_iota(jnp.int32, cq_ref.shape, 1)
    cq = jnp.sum(jnp.where(lane == h, cq_ref[...], 0.0), axis=-1, keepdims=True) * LOG2E

    def ck(kb):
        return ck_ref[0, pl.ds(kb, 1), :] * LOG2E

    def scores(kb):
        return _dot_nt(q, k_ref[pl.ds(pl.multiple_of(kb * tk, tk), tk), :])

    def pv_dot(kb, slot):
        return _dot(p_scr[slot], v_ref[pl.ds(pl.multiple_of(kb * tk, tk), tk), :])

    def fold(pv):
        acc_ref[...] = alpha_ref[...] * acc_ref[...] + pv

    def softmax(t, slot, between):
        m_prev = m_ref[...]
        m_new = jnp.maximum(m_prev, jnp.max(t, axis=-1, keepdims=True) + cq)
        between()
        alpha = jnp.exp2(m_prev - m_new)
        p = jnp.exp2(t + (cq - m_new))
        l_ref[...] = alpha * l_ref[...] + jnp.sum(p, axis=-1, keepdims=True)
        alpha_ref[...] = alpha
        p_scr[slot] = p.astype(p_scr.dtype)
        m_ref[...] = m_new

    m_ref[...] = jnp.full(m_ref.shape, -jnp.inf, F32)
    l_ref[...] = jnp.zeros(l_ref.shape, F32)
    alpha_ref[...] = jnp.zeros(alpha_ref.shape, F32)
    acc_ref[...] = jnp.zeros(acc_ref.shape, F32)
    p_scr[1] = jnp.zeros(p_scr.shape[1:], p_scr.dtype)
    s_scr[0] = scores(0)

    def body(k, carry):
        cur = k % 2
        pv = pv_dot(jnp.maximum(k - 1, 0), 1 - cur)
        s_next = scores(k + 1)
        t = s_scr[cur] - ck(k)

        def between():
            fold(pv)
            s_scr[1 - cur] = s_next

        softmax(t, cur, between)
        return carry

    lax.fori_loop(0, qi, body, 0)

    cur = qi % 2
    pv = pv_dot(jnp.maximum(qi - 1, 0), 1 - cur)
    causal = (lax.broadcasted_iota(jnp.int32, (tq, tk), 0) >= lax.broadcasted_iota(jnp.int32, (tq, tk), 1))
    t = jnp.where(causal, s_scr[cur] - ck(qi), -jnp.inf)
    softmax(t, cur, lambda: fold(pv))
    fold(pv_dot(qi, cur))
    o = acc_ref[...] / l_ref[...]
    o_ref[...] = (o * _sigmoid(og_ref[...].astype(F32))).astype(o_ref.dtype)


def fox_attention(proj, cum, cum_rows, *, batch, heads, tq=512):
    T = proj.shape[0]
    S = T // batch
    nq = S // tq
    H = heads
    kern = functools.partial(_fox_attn_kernel, tq=tq, nq=nq)
    return pl.pallas_call(
        kern,
        grid=(batch, H, nq),
        in_specs=[
            pl.BlockSpec((tq, HEAD_DIM), lambda b, h, qi: (b * nq + qi, h)),
            pl.BlockSpec((S, HEAD_DIM), lambda b, h, qi: (b, H + h)),
            pl.BlockSpec((S, HEAD_DIM), lambda b, h, qi: (b, 2 * H + h)),
            pl.BlockSpec((tq, HEAD_DIM), lambda b, h, qi: (b * nq + qi, 3 * H + h)),
            pl.BlockSpec((tq, LANES), lambda b, h, qi: (b * nq + qi, 0)),
            pl.BlockSpec((1, nq, tq), lambda b, h, qi: (b * H + h, 0, 0)),
        ],
        out_specs=pl.BlockSpec((tq, HEAD_DIM), lambda b, h, qi: (b * nq + qi, h)),
        out_shape=jax.ShapeDtypeStruct((T, H * HEAD_DIM), BF16),
        scratch_shapes=[pltpu.VMEM((2, tq, tq), F32), pltpu.VMEM((2, tq, tq), BF16),
                        pltpu.VMEM((tq, 1), F32), pltpu.VMEM((tq, 1), F32), pltpu.VMEM((tq, 1), F32),
                        pltpu.VMEM((tq, HEAD_DIM), F32)],
        compiler_params=_params("parallel", "parallel", "arbitrary"),
        name="fox_attention",
    )(proj, proj, proj, proj, cum, cum_rows)


def _gdn_kernel(q_ref, k_ref, v_ref, z_ref, small_ref, grow_ref, gain_ref,
                o_ref, cols_ref, u_ref, w_ref, a_ref, o_scr, *, hv, nk, group):
    C = GDN_CHUNK
    S = q_ref.shape[0]
    n_chunks = S // C
    nv = 2 * nk
    hp = pl.program_id(1)
    cols = lambda i: slice(i * HEAD_DIM, (i + 1) * HEAD_DIM)

    cols_ref[...] = pltpu.roll(small_ref[...], (LANES - nv * hp) % LANES, 1)

    ri = lax.broadcasted_iota(jnp.int32, (C, C), 0)
    ci = lax.broadcasted_iota(jnp.int32, (C, C), 1)
    incl = ri >= ci
    strict = ri > ci
    eye = (ri == ci).astype(F32)
    steps = int(math.log2(C)) - 1

    def prep_stages(gi):
        chunk_rows = [pl.ds(pl.multiple_of((gi * group + t) * C, C), C) for t in range(group)]
        kf, qkk = {}, {}
        for t, rows in enumerate(chunk_rows):
            for kh in range(nk):
                kc = k_ref[rows, cols(kh)]
                qc = q_ref[rows, cols(kh)]
                kf[t, kh] = kc.astype(F32)
                qkk[t, kh] = _dot_nt(jnp.concatenate([kc, qc], axis=0), kc)
        yield
        chains = [(t, h) for t in range(group) for h in range(nv)]
        beta, g, x, lb = {}, {}, {}, {}
        for t, h in chains:
            rows = chunk_rows[t]
            g[t, h] = cols_ref[rows, hv + h:hv + h + 1]
            beta[t, h] = cols_ref[rows, h:h + 1]
            grow = grow_ref[h, pl.ds(gi * group + t, 1), :]
            decay = jnp.exp(jnp.where(incl, g[t, h] - grow, -jnp.inf))
            lmat = jnp.where(strict, beta[t, h] * qkk[t, h // 2][:C] * decay, 0.0)
            a_ref[h, rows, :] = jnp.where(incl, qkk[t, h // 2][C:] * decay, 0.0)
            x[t, h] = eye - lmat
            lb[t, h] = lmat.astype(BF16)
        p = {ch: _dot(lb[ch], lb[ch]) for ch in chains}
        yield
        for s in range(steps):
            if s < steps - 1:
                px = {ch: _dot(jnp.concatenate([p[ch], x[ch]], axis=0).astype(BF16), p[ch].astype(BF16))
                      for ch in chains}
                for ch in chains:
                    p[ch], x[ch] = px[ch][:C], x[ch] + px[ch][C:]
            else:
                xp = {ch: _dot(x[ch].astype(BF16), p[ch].astype(BF16)) for ch in chains}
                for ch in chains:
                    x[ch] = x[ch] + xp[ch]
            yield
        sol = {}
        for t, h in chains:
            vf = v_ref[chunk_rows[t], cols(h)].astype(F32)
            rhs = jnp.concatenate([vf * beta[t, h], kf[t, h // 2] * (beta[t, h] * jnp.exp(g[t, h]))], axis=1)
            sol[t, h] = _dot(x[t, h].astype(BF16), rhs.astype(BF16))
        for t, h in chains:
            u_ref[h, chunk_rows[t], :] = sol[t, h][:, :HEAD_DIM]
            w_ref[h, chunk_rows[t], :] = sol[t, h][:, HEAD_DIM:]
        yield

    def rec_stages(gi, states):
        for t in range(group):
            c = gi * group + t
            rows = pl.ds(pl.multiple_of(c * C, C), C)
            last = pl.ds(c * C + C - 1, 1)
            kf = [k_ref[rows, cols(kh)].astype(F32) for kh in range(nk)]
            qf = [q_ref[rows, cols(kh)].astype(F32) for kh in range(nk)]
            g = [cols_ref[rows, hv + h:hv + h + 1] for h in range(nv)]
            g_last = [cols_ref[last, hv + h:hv + h + 1] for h in range(nv)]
            ws_qs = []
            for h in range(nv):
                wq = jnp.concatenate([w_ref[h, rows, :], qf[h // 2] * jnp.exp(g[h])], axis=0)
                ws_qs.append(_dot(wq.astype(BF16), states[h].astype(BF16)))
            yield
            vb = [(u_ref[h, rows, :] - ws_qs[h][:C]).astype(BF16) for h in range(nv)]
            av = [_dot(a_ref[h, rows, :].astype(BF16), vb[h]) for h in range(nv)]
            kv = [_dot_tn((kf[h // 2] * jnp.exp(g_last[h] - g[h])).astype(BF16), vb[h]) for h in range(nv)]
            for h in range(nv):
                o_scr[h, rows, :] = ws_qs[h][C:] + av[h]
                states[h] = states[h] * jnp.exp(g_last[h]) + kv[h]
            yield

    def run_interleaved(*gens):
        live = list(gens)
        while live:
            for gen in list(live):
                if next(gen, "done") == "done":
                    live.remove(gen)

    n_groups = n_chunks // group
    run_interleaved(prep_stages(0))

    def body(gi, states):
        states = list(states)
        run_interleaved(rec_stages(gi, states), prep_stages(gi + 1))
        return tuple(states)

    zero = jnp.zeros((HEAD_DIM, HEAD_DIM), F32)
    states = list(lax.fori_loop(0, n_groups - 1, body, (zero,) * nv))
    run_interleaved(rec_stages(n_groups - 1, states))

    for h in range(nv):
        o = _rms(o_scr[h], gain_ref[...])
        z = z_ref[:, cols(h)].astype(F32)
        o_ref[:, cols(h)] = (o * _silu(z)).astype(o_ref.dtype)


def gated_delta(proj, small, grow, o_gain, *, batch, k_heads, nk=2, group=4):
    T = proj.shape[0]
    S = T // batch
    HK = k_heads
    HV = 2 * HK
    C = GDN_CHUNK
    n_chunks = S // C
    nk = min(nk, HK)
    nv = 2 * nk
    NP = HK // nk
    group = min(group, n_chunks)
    kern = functools.partial(_gdn_kernel, hv=HV, nk=nk, group=group)
    kw = nk * HEAD_DIM
    vw = nv * HEAD_DIM
    return pl.pallas_call(
        kern,
        grid=(batch, NP),
        in_specs=[
            pl.BlockSpec((S, kw), lambda b, h: (b, h)),
            pl.BlockSpec((S, kw), lambda b, h: (b, NP + h)),
            pl.BlockSpec((S, vw), lambda b, h: (b, NP + h)),
            pl.BlockSpec((S, vw), lambda b, h: (b, 2 * NP + h)),
            pl.BlockSpec((S, LANES), lambda b, h: (b, 0)),
            pl.BlockSpec((nv, n_chunks, C), lambda b, h: (b * NP + h, 0, 0)),
            pl.BlockSpec((1, HEAD_DIM), lambda b, h: (0, 0)),
        ],
        out_specs=pl.BlockSpec((S, vw), lambda b, h: (b, h)),
        out_shape=jax.ShapeDtypeStruct((T, HV * HEAD_DIM), BF16),
        scratch_shapes=[pltpu.VMEM((S, LANES), F32),
                        pltpu.VMEM((nv, S, HEAD_DIM), F32), pltpu.VMEM((nv, S, HEAD_DIM), F32),
                        pltpu.VMEM((nv, S, C), F32), pltpu.VMEM((nv, S, HEAD_DIM), F32)],
        compiler_params=_params("parallel", "parallel"),
        name="gated_delta",
    )(proj, proj, proj, proj, small, grow, o_gain.reshape(1, HEAD_DIM))


def _row_copy(src_ref, dst_ref, src_row, dst_row, sem):
    return pltpu.make_async_copy(src_ref.at[pl.ds(src_row, 1), :], dst_ref.at[pl.ds(dst_row, 1), :], sem)


def _gather_rows_kernel(idx_ref, src_ref, o_ref, buf_ref, sem, *, rows):
    t = pl.program_id(0)
    slot = t % 2

    def issue(step, sl):
        base = step * rows

        def start(g, c):
            for u in range(DMA_ISSUE_UNROLL):
                r = g * DMA_ISSUE_UNROLL + u
                _row_copy(src_ref, buf_ref.at[sl], idx_ref[base + r], r, sem.at[sl]).start(priority=u % 2)
            return c

        lax.fori_loop(0, rows // DMA_ISSUE_UNROLL, start, 0)

    @pl.when(t == 0)
    def _():
        issue(0, 0)

    @pl.when(t + 1 < pl.num_programs(0))
    def _():
        issue(t + 1, 1 - slot)

    def wait(r, c):
        _row_copy(src_ref, buf_ref.at[slot], 0, r, sem.at[slot]).wait()
        return c

    lax.fori_loop(0, rows, wait, 0, unroll=8)
    o_ref[...] = buf_ref[slot].astype(o_ref.dtype)


def gather_rows(src, idx, *, rows=256):
    P = idx.shape[0]
    D = src.shape[1]
    return pl.pallas_call(
        functools.partial(_gather_rows_kernel, rows=rows),
        grid_spec=pltpu.PrefetchScalarGridSpec(
            num_scalar_prefetch=1,
            grid=(P // rows,),
            in_specs=[pl.BlockSpec(memory_space=pl.ANY)],
            out_specs=pl.BlockSpec((rows, D), lambda t, idx: (t, 0)),
            scratch_shapes=[pltpu.VMEM((2, rows, D), F32), pltpu.SemaphoreType.DMA((2,))]),
        out_shape=jax.ShapeDtypeStruct((P, D), BF16),
        compiler_params=_params("arbitrary"),
        name="moe_gather",
    )(idx, src)


def _combine_kernel(pos_ref, y_ref, h_ref, small_ref, o_ref, buf_ref, sem, *, rows):
    t = pl.program_id(0)
    slot = t % 2

    def issue(step, sl):
        base = step * rows

        def start(g, c):
            for u in range(DMA_ISSUE_UNROLL // 2):
                r = g * (DMA_ISSUE_UNROLL // 2) + u
                for k in range(2):
                    _row_copy(y_ref, buf_ref.at[sl, k], pos_ref[k, base + r], r, sem.at[sl]).start(priority=k)
            return c

        lax.fori_loop(0, rows // (DMA_ISSUE_UNROLL // 2), start, 0)

    @pl.when(t == 0)
    def _():
        issue(0, 0)

    @pl.when(t + 1 < pl.num_programs(0))
    def _():
        issue(t + 1, 1 - slot)

    def wait(r, c):
        for k in range(2):
            _row_copy(y_ref, buf_ref.at[slot, k], 0, r, sem.at[slot]).wait()
        return c

    lax.fori_loop(0, rows, wait, 0, unroll=4)
    w1 = small_ref[:, 2:3]
    w2 = small_ref[:, 3:4]
    o_ref[...] = h_ref[...] + (w1 * buf_ref[slot, 0] + w2 * buf_ref[slot, 1])


def moe_combine(y, pos, h, small, *, rows=256):
    T, D = h.shape
    return pl.pallas_call(
        functools.partial(_combine_kernel, rows=rows),
        grid_spec=pltpu.PrefetchScalarGridSpec(
            num_scalar_prefetch=1,
            grid=(T // rows,),
            in_specs=[pl.BlockSpec(memory_space=pl.ANY),
                      pl.BlockSpec((rows, D), lambda t, pos: (t, 0)),
                      pl.BlockSpec((rows, LANES), lambda t, pos: (t, 0))],
            out_specs=pl.BlockSpec((rows, D), lambda t, pos: (t, 0)),
            scratch_shapes=[pltpu.VMEM((2, 2, rows, D), F32), pltpu.SemaphoreType.DMA((2,))]),
        out_shape=jax.ShapeDtypeStruct((T, D), F32),
        compiler_params=_params("arbitrary"),
        name="moe_combine",
    )(pos, y, h, small)


def _grouped_kernel(te_ref, nv_ref, nxt_ref, x_ref, *rest, n_w, layer, tn, finish):
    w_hbm = rest[:n_w]
    o_ref = rest[n_w]
    stage = rest[n_w + 1:2 * n_w + 1]
    wbf = rest[2 * n_w + 1:3 * n_w + 1]
    sem = rest[3 * n_w + 1]
    j = pl.program_id(0)
    t = pl.program_id(1)
    e = te_ref[t]
    first = (t == 0) | (e != te_ref[jnp.maximum(t - 1, 0)])
    valid = t < nv_ref[0]

    def copies(jj, ee):
        cols = pl.ds(pl.multiple_of(jj * tn, LANES), tn)
        return [pltpu.make_async_copy(w_hbm[i].at[layer, ee, :, cols], stage[i], sem.at[i])
                for i in range(n_w)]

    @pl.when((j == 0) & (t == 0))
    def _():
        for c in copies(j, e):
            c.start()

    @pl.when(valid & first)
    def _():
        for i, c in enumerate(copies(j, e)):
            c.wait()
            wbf[i][...] = stage[i][...].astype(BF16)
        nt = nxt_ref[t]

        @pl.when(nt >= 0)
        def _():
            for c in copies(j, te_ref[jnp.maximum(nt, 0)]):
                c.start()

        @pl.when((nt < 0) & (j + 1 < pl.num_programs(0)))
        def _():
            for c in copies(j + 1, te_ref[0]):
                c.start()

    @pl.when(valid)
    def _():
        x = x_ref[...]
        o_ref[...] = finish([_dot(x, w[...]) for w in wbf]).astype(o_ref.dtype)

    @pl.when(jnp.logical_not(valid))
    def _():
        o_ref[...] = jnp.zeros(o_ref.shape, o_ref.dtype)


def grouped_matmul(x, weights, layer, tile_expert, n_valid, next_tile, *, tm, tn, out_dtype, finish, name):
    P, K = x.shape
    N = weights[0].shape[3]
    n_w = len(weights)
    last = lambda t, nv: jnp.minimum(t, nv[0] - 1)
    return pl.pallas_call(
        functools.partial(_grouped_kernel, n_w=n_w, layer=layer, tn=tn, finish=finish),
        grid_spec=pltpu.PrefetchScalarGridSpec(
            num_scalar_prefetch=3,
            grid=(N // tn, P // tm),
            in_specs=[pl.BlockSpec((tm, K), lambda j, t, te, nv, nx: (last(t, nv), 0)),
                      *[pl.BlockSpec(memory_space=pl.ANY)] * n_w],
            out_specs=pl.BlockSpec((tm, tn), lambda j, t, te, nv, nx: (t, j)),
            scratch_shapes=[*[pltpu.VMEM((K, tn), F32)] * n_w, *[pltpu.VMEM((K, tn), BF16)] * n_w,
                            pltpu.SemaphoreType.DMA((n_w,))]),
        out_shape=jax.ShapeDtypeStruct((P, N), out_dtype),
        compiler_params=_params("arbitrary", "arbitrary"),
        name=name,
    )(tile_expert, n_valid, next_tile, x, *weights)


def _moe_plan(small, *, tm):
    T = small.shape[0]
    experts = small[:, 0:2].astype(jnp.int32)
    flat = experts.T.reshape(-1)
    onehot = (flat[:, None] == jnp.arange(N_EXPERTS)[None, :]).astype(jnp.int32)
    rank = jnp.take_along_axis(jnp.cumsum(onehot, axis=0) - onehot, flat[:, None], axis=1)[:, 0]
    counts = jnp.sum(onehot, axis=0)
    tiles_per = (counts + tm - 1) // tm
    tile_end = jnp.cumsum(tiles_per)
    tile_start = tile_end - tiles_per
    pos = tile_start[flat] * tm + rank
    n_tiles = (2 * T) // tm + N_EXPERTS
    n_valid = tile_end[-1]
    tile_ids = jnp.minimum(jnp.arange(n_tiles), n_valid - 1)
    tile_expert = jnp.sum((tile_ids[:, None] >= tile_end[None, :]).astype(jnp.int32), axis=1)
    token = jnp.tile(jnp.arange(T, dtype=jnp.int32), 2)
    row_token = jnp.zeros((n_tiles * tm,), jnp.int32).at[pos].set(token)
    next_tile = jnp.where(tile_end[tile_expert] < n_valid, tile_end[tile_expert], -1)
    return (pos.reshape(2, T).astype(jnp.int32), row_token, tile_expert.astype(jnp.int32),
            n_valid.reshape(1).astype(jnp.int32), next_tile.astype(jnp.int32))


def moe_ffn(h, hn, small, wg, wu, wd, layer, *, tm=512):
    tm = min(tm, h.shape[0])
    pos, row_token, tile_expert, n_valid, next_tile = _moe_plan(small, tm=tm)
    xs = gather_rows(hn, row_token, rows=min(256, tm))
    N = wg.shape[3]
    tn_up = N // 2 if (N // 2) % LANES == 0 else N
    mid = grouped_matmul(xs, (wg, wu), layer, tile_expert, n_valid, next_tile, tm=tm, tn=tn_up,
                         out_dtype=BF16, finish=lambda d: _silu(d[0]) * d[1], name="expert_up")
    y = grouped_matmul(mid, (wd,), layer, tile_expert, n_valid, next_tile, tm=tm,
                       tn=min(1024, wd.shape[3]), out_dtype=F32, finish=lambda d: d[0], name="expert_down")
    return moe_combine(y, pos, h, small, rows=min(256, tm))


def _fox_layer(h, batch, j, norm_mix, norm_ffn, w_in_t, b_f, q_gain, k_gain, w_out, w_gate, w_up,
               w_down):
    T, D = h.shape
    S = T // batch
    H = b_f.shape[0]
    width = H * HEAD_DIM
    hn, lf = rmsnorm_small(h, norm_mix, _pad_cols(w_in_t[j, 4 * width:, :].T), _aux_rows(b_f), _post_fox,
                           name="fox_norm")
    cum = seq_cumsum(lf, batch)
    tq = min(512, S)
    cum_rows = cum[:, :H].reshape(batch, S, H).transpose(0, 2, 1).reshape(batch * H, S // tq, tq)
    tn = min(1024, width)
    proj = matmul(hn, w_in_t, j, n_cols=4 * width, tm=min(1024, T), tn=tn, out_dtype=BF16, w_transposed=True,
                  variants=functools.partial(_var_fox_in, n_q_blocks=width // tn),
                  extra=(q_gain.reshape(1, HEAD_DIM), k_gain.reshape(1, HEAD_DIM)),
                  extra_specs=(pl.BlockSpec((1, HEAD_DIM), lambda j, i: (0, 0)),) * 2,
                  name="fox_in_proj")
    o = fox_attention(proj, cum, cum_rows, batch=batch, heads=H, tq=tq)
    h = matmul_residual(o, w_out, j, h, tm=min(1024, T), tn=min(1024, D), name="fox_out_proj")
    hn2 = rmsnorm(h, norm_ffn)
    F = w_gate.shape[2]
    mid = swiglu_up(hn2, w_gate, w_up, j, tm=min(1024, T), tn=512 if F % 512 == 0 else F)
    return matmul_residual(mid, w_down, j, h, tm=min(512, T), tn=min(512, D), name="ffn_down")


def _gdn_layer(h, batch, j, norm_mix, norm_ffn, w_in_t, conv_w, a_log, dt_bias, o_gain, w_out,
               router, wg, wu, wd, *, seq_tile=1024):
    T, D = h.shape
    S = T // batch
    HV = a_log.shape[0]
    HK = HV // 2
    kw = HK * HEAD_DIM
    conv_ch = 4 * kw
    main = conv_ch + 2 * kw
    zeros = jnp.zeros((HV,), F32)
    aux = _aux_rows(jnp.concatenate([zeros, a_log]), jnp.concatenate([zeros, dt_bias]))
    hn, small = rmsnorm_small(h, norm_mix, _pad_cols(w_in_t[j, main:, :].T), aux,
                              functools.partial(_post_gdn, hv=HV), name="gdn_norm")
    small = chunk_cumsum(small, chunk=min(GDN_CHUNK, S), lo=HV, hi=2 * HV)
    C = GDN_CHUNK
    grow = small[:, HV:2 * HV].reshape(batch, S, HV).transpose(0, 2, 1).reshape(batch * HV, S // C, C)
    tn = min(1024, kw)
    tm = min(seq_tile, S)
    proj = matmul(hn, w_in_t, j, n_cols=main, tm=tm, tn=tn, out_dtype=BF16, w_transposed=True,
                  variants=functools.partial(_var_gdn_in, n_k_blocks=kw // tn,
                                             n_conv_blocks=conv_ch // tn, rows_per_seq=S // tm,
                                             q_scale=HEAD_DIM ** -0.5),
                  extra=(conv_w,),
                  extra_specs=(pl.BlockSpec((GDN_CONV, tn),
                                            lambda j, i: (0, jnp.minimum(j, conv_ch // tn - 1))),),
                  extra_scratch=(pltpu.VMEM((8 + tm, tn), F32),),
                  name="gdn_in_proj")
    o = gated_delta(proj, small, grow, o_gain, batch=batch, k_heads=HK)
    h = matmul_residual(o, w_out, j, h, tm=min(1024, T), tn=min(512, D), name="gdn_out_proj")
    hn2, route = rmsnorm_small(h, norm_ffn, _pad_cols(router), _aux_rows(), _post_moe,
                               hn_dtype=F32, name="moe_norm")
    return moe_ffn(h, hn2, route, wg, wu, wd, j)


def kernel(x, norm_mix, norm_ffn, fox_w_in, fox_b_f, fox_q_norm, fox_k_norm, fox_w_out,
           gdn_w_in, gdn_conv, gdn_a_log, gdn_dt_bias, gdn_o_norm, gdn_w_out,
           ffn_w_gate, ffn_w_up, ffn_w_down,
           moe_router, moe_w_gate, moe_w_up, moe_w_down):
    B, S, D = x.shape
    h = x.reshape(B * S, D)
    depth = norm_mix.shape[0]
    fox_w_in_t = jnp.swapaxes(fox_w_in, 1, 2)
    gdn_w_in_t = jnp.swapaxes(gdn_w_in, 1, 2)
    for i in range(depth):
        j = i // 2
        if i % 2 == 0:
            h = _fox_layer(h, B, j, norm_mix[i], norm_ffn[i], fox_w_in_t, fox_b_f[j], fox_q_norm[j],
                           fox_k_norm[j], fox_w_out, ffn_w_gate, ffn_w_up, ffn_w_down)
        else:
            h = _gdn_layer(h, B, j, norm_mix[i], norm_ffn[i], gdn_w_in_t, gdn_conv[j], gdn_a_log[j],
                           gdn_dt_bias[j], gdn_o_norm[j], gdn_w_out, moe_router[j],
                           moe_w_gate, moe_w_up, moe_w_down)
    return h.reshape(B, S, D)
```

```python
import functools
import math

import jax
import jax.numpy as jnp
from jax import lax
from jax.experimental import pallas as pl
from jax.experimental.pallas import tpu as pltpu

F32 = jnp.float32
BF16 = jnp.bfloat16
EPS = 1e-6
LANES = 128
HEAD_DIM = 128
GDN_CHUNK = 64
GDN_CONV = 4
N_EXPERTS = 8
VMEM_LIMIT_BYTES = 56 * 1024 * 1024
HIGHEST = lax.Precision.HIGHEST
LOG2E = math.log2(math.e)
DMA_ISSUE_UNROLL = 8


def _params(*sem):
    return pltpu.CompilerParams(dimension_semantics=sem, vmem_limit_bytes=VMEM_LIMIT_BYTES)


def _log1pexp_neg_abs(x):
    return jnp.log(1.0 + jnp.exp(-jnp.abs(x)))


def _log_sigmoid(x):
    return jnp.minimum(x, 0.0) - _log1pexp_neg_abs(x)


def _softplus(x):
    return jnp.maximum(x, 0.0) + _log1pexp_neg_abs(x)


def _sigmoid(x):
    return 1.0 / (1.0 + jnp.exp(-x))


def _silu(x):
    return x * _sigmoid(x)


def _dot(a, b):
    return jnp.dot(a, b, preferred_element_type=F32)


def _dot_nt(a, b):
    return lax.dot_general(a, b, (((1,), (1,)), ((), ())), preferred_element_type=F32)


def _dot_tn(a, b):
    return lax.dot_general(a, b, (((0,), (0,)), ((), ())), preferred_element_type=F32)


def _rms(x, g):
    ms = jnp.mean(x * x, axis=-1, keepdims=True)
    return x * lax.rsqrt(ms + EPS) * g


def _norm_kernel(x_ref, g_ref, hn_ref):
    hn_ref[...] = _rms(x_ref[...], g_ref[...]).astype(hn_ref.dtype)


def _norm_small_kernel(x_ref, g_ref, ws_ref, aux_ref, hn_ref, small_ref, *, post):
    y = _rms(x_ref[...], g_ref[...])
    hn_ref[...] = y.astype(hn_ref.dtype)
    s = jnp.dot(y, ws_ref[...], preferred_element_type=F32, precision=HIGHEST)
    small_ref[...] = post(s, aux_ref[...])


def rmsnorm(x, g, *, tm=512):
    T, D = x.shape
    return pl.pallas_call(
        _norm_kernel,
        grid=(T // tm,),
        in_specs=[pl.BlockSpec((tm, D), lambda i: (i, 0)),
                  pl.BlockSpec((1, D), lambda i: (0, 0))],
        out_specs=pl.BlockSpec((tm, D), lambda i: (i, 0)),
        out_shape=jax.ShapeDtypeStruct((T, D), BF16),
        compiler_params=_params("parallel"),
        name="rmsnorm",
    )(x, g.reshape(1, D))


def rmsnorm_small(x, g, w_small, aux, post, *, hn_dtype=BF16, tm=512, name="rmsnorm_small"):
    T, D = x.shape
    return pl.pallas_call(
        functools.partial(_norm_small_kernel, post=post),
        grid=(T // tm,),
        in_specs=[pl.BlockSpec((tm, D), lambda i: (i, 0)),
                  pl.BlockSpec((1, D), lambda i: (0, 0)),
                  pl.BlockSpec((D, LANES), lambda i: (0, 0)),
                  pl.BlockSpec((8, LANES), lambda i: (0, 0))],
        out_specs=[pl.BlockSpec((tm, D), lambda i: (i, 0)),
                   pl.BlockSpec((tm, LANES), lambda i: (i, 0))],
        out_shape=[jax.ShapeDtypeStruct((T, D), hn_dtype),
                   jax.ShapeDtypeStruct((T, LANES), F32)],
        compiler_params=_params("parallel"),
        name=name,
    )(x, g.reshape(1, D), w_small, aux)


def _pad_cols(w, n=LANES):
    return jnp.pad(w, ((0, 0), (0, n - w.shape[1])))


def _aux_rows(*rows):
    out = [jnp.pad(r.astype(F32), (0, LANES - r.shape[0])) for r in rows]
    out += [jnp.zeros((LANES,), F32)] * (8 - len(out))
    return jnp.stack(out)


def _post_fox(s, aux):
    return _log_sigmoid(s + aux[0:1, :])


def _post_gdn(s, aux, *, hv):
    lane = lax.broadcasted_iota(jnp.int32, s.shape, 1)
    beta = _sigmoid(s)
    g = -jnp.exp(aux[0:1, :]) * _softplus(s + aux[1:2, :])
    return jnp.where(lane < hv, beta, g)


def _post_moe(s, aux):
    del aux
    lane = lax.broadcasted_iota(jnp.int32, s.shape, 1).astype(F32)
    neg = jnp.float32(-jnp.inf)
    logits = jnp.where(lane < N_EXPERTS, s, neg)
    v1 = jnp.max(logits, axis=-1, keepdims=True)
    i1 = jnp.min(jnp.where(logits == v1, lane, float(LANES)), axis=-1, keepdims=True)
    rest = jnp.where(lane == i1, neg, logits)
    v2 = jnp.max(rest, axis=-1, keepdims=True)
    i2 = jnp.min(jnp.where(rest == v2, lane, float(LANES)), axis=-1, keepdims=True)
    e2 = jnp.exp(v2 - v1)
    denom = 1.0 + e2
    out = jnp.where(lane == 0.0, i1, 0.0)
    out = jnp.where(lane == 1.0, i2, out)
    out = jnp.where(lane == 2.0, 1.0 / denom, out)
    out = jnp.where(lane == 3.0, e2 / denom, out)
    return out


def _cumsum_kernel(x_ref, o_ref, *, blk):
    S = x_ref.shape[0]
    r = lax.broadcasted_iota(jnp.int32, (blk, blk), 0)
    c = lax.broadcasted_iota(jnp.int32, (blk, blk), 1)
    tri = (r >= c).astype(F32)
    carry = jnp.zeros((1, LANES), F32)
    for i in range(S // blk):
        cs = jnp.dot(tri, x_ref[i * blk:(i + 1) * blk, :], preferred_element_type=F32,
                     precision=HIGHEST) + carry
        o_ref[i * blk:(i + 1) * blk, :] = cs
        carry = cs[blk - 1:blk, :]


def seq_cumsum(x, batch, *, blk=128):
    T = x.shape[0]
    S = T // batch
    blk = min(blk, S)
    return pl.pallas_call(
        functools.partial(_cumsum_kernel, blk=blk),
        grid=(batch,),
        in_specs=[pl.BlockSpec((S, LANES), lambda b: (b, 0))],
        out_specs=pl.BlockSpec((S, LANES), lambda b: (b, 0)),
        out_shape=jax.ShapeDtypeStruct((T, LANES), F32),
        compiler_params=_params("parallel"),
        name="seq_cumsum",
    )(x)


def _chunk_cumsum_kernel(x_ref, o_ref, *, chunk, lo, hi):
    R = x_ref.shape[0]
    r = lax.broadcasted_iota(jnp.int32, (chunk, chunk), 0)
    c = lax.broadcasted_iota(jnp.int32, (chunk, chunk), 1)
    tri = (r >= c).astype(F32)
    lane = lax.broadcasted_iota(jnp.int32, (chunk, LANES), 1)
    sel = (lane >= lo) & (lane < hi)
    for i in range(R // chunk):
        x = x_ref[i * chunk:(i + 1) * chunk, :]
        cs = jnp.dot(tri, x, preferred_element_type=F32, precision=HIGHEST)
        o_ref[i * chunk:(i + 1) * chunk, :] = jnp.where(sel, cs, x)


def chunk_cumsum(x, *, chunk, lo, hi, tm=512):
    T = x.shape[0]
    tm = min(tm, T)
    return pl.pallas_call(
        functools.partial(_chunk_cumsum_kernel, chunk=chunk, lo=lo, hi=hi),
        grid=(T // tm,),
        in_specs=[pl.BlockSpec((tm, LANES), lambda i: (i, 0))],
        out_specs=pl.BlockSpec((tm, LANES), lambda i: (i, 0)),
        out_shape=jax.ShapeDtypeStruct((T, LANES), F32),
        compiler_params=_params("parallel"),
        name="chunk_cumsum",
    )(x)


def _mm_kernel(a_ref, w_ref, *rest, n_extra, variants, n_sub, w_transposed):
    extra = rest[:n_extra]
    o_ref = rest[n_extra]
    scratch = rest[n_extra + 1:]
    wbf_ref = scratch[0]

    @pl.when(pl.program_id(1) == 0)
    def _():
        w = w_ref[...]
        wbf_ref[...] = (w.T if w_transposed else w).astype(BF16)

    rs = a_ref.shape[0] // n_sub
    for cond, fn in variants(pl.program_id(0), pl.program_id(1), extra, o_ref, scratch[1:]):
        @pl.when(cond)
        def _(fn=fn):
            pending = None
            for sub in range(n_sub):
                rows = slice(sub * rs, (sub + 1) * rs)
                acc = _dot(a_ref[rows, :], wbf_ref[...])
                if pending is not None:
                    fn(*pending)
                pending = (acc, rows)
            fn(*pending)


def _layer_w_spec(K, tn, layer):
    return pl.BlockSpec((None, K, tn), lambda j, i: (layer, 0, j))


def matmul(a, w, layer, *, n_cols, tm, tn, out_dtype, variants, extra=(), extra_specs=(),
           extra_scratch=(), n_sub=4, w_transposed=False, name="matmul"):
    M, K = a.shape
    grid = (n_cols // tn, M // tm)
    w_spec = (pl.BlockSpec((None, tn, K), lambda j, i: (layer, j, 0)) if w_transposed
              else _layer_w_spec(K, tn, layer))
    return pl.pallas_call(
        functools.partial(_mm_kernel, n_extra=len(extra), variants=variants, n_sub=n_sub,
                          w_transposed=w_transposed),
        grid=grid,
        in_specs=[pl.BlockSpec((tm, K), lambda j, i: (i, 0)),
                  w_spec,
                  *extra_specs],
        out_specs=pl.BlockSpec((tm, tn), lambda j, i: (i, j)),
        out_shape=jax.ShapeDtypeStruct((M, n_cols), out_dtype),
        scratch_shapes=[pltpu.VMEM((K, tn), BF16), *extra_scratch],
        compiler_params=_params("arbitrary", "arbitrary"),
        name=name,
    )(a, w, *extra)


def _var_residual(j, i, extra, o_ref, scratch):
    del j, i, scratch

    def fn(acc, rows):
        o_ref[rows, :] = extra[0][rows, :] + acc

    return [(True, fn)]


def matmul_residual(a, w, layer, res, *, tm, tn, name):
    N = w.shape[2]
    return matmul(a, w, layer, n_cols=N, tm=tm, tn=tn, out_dtype=F32, variants=_var_residual,
                  extra=(res,), extra_specs=(pl.BlockSpec((tm, tn), lambda j, i: (i, j)),),
                  name=name)


def _store_cast(o_ref):
    def fn(acc, rows):
        o_ref[rows, :] = acc.astype(o_ref.dtype)
    return fn


def _var_fox_in(j, i, extra, o_ref, scratch, *, n_q_blocks):
    del i, scratch
    qg_ref, kg_ref = extra
    tn = o_ref.shape[1]

    def normed(gain_ref, scale):
        def fn(acc, rows):
            for h in range(tn // HEAD_DIM):
                cols = slice(h * HEAD_DIM, (h + 1) * HEAD_DIM)
                o_ref[rows, cols] = (_rms(acc[:, cols], gain_ref[...]) * scale).astype(o_ref.dtype)
        return fn

    return [(j < n_q_blocks, normed(qg_ref, HEAD_DIM ** -0.5 * LOG2E)),
            ((j >= n_q_blocks) & (j < 2 * n_q_blocks), normed(kg_ref, 1.0)),
            (j >= 2 * n_q_blocks, _store_cast(o_ref))]


def _var_gdn_in(j, i, extra, o_ref, scratch, *, n_k_blocks, n_conv_blocks, rows_per_seq, q_scale):
    conv_ref, = extra
    buf_ref, = scratch
    tm, tn = o_ref.shape

    @pl.when(i % rows_per_seq == 0)
    def _():
        buf_ref[0:8, :] = jnp.zeros((8, tn), F32)

    def conv(l2norm):
        def fn(acc, rows):
            r0, r1 = rows.start, rows.stop
            buf_ref[8 + r0:8 + r1, :] = acc
            y = acc * conv_ref[GDN_CONV - 1:GDN_CONV, :]
            for tap in range(GDN_CONV - 1):
                shift = GDN_CONV - 1 - tap
                y = y + buf_ref[8 + r0 - shift:8 + r1 - shift, :] * conv_ref[tap:tap + 1, :]
            if r1 == tm:
                buf_ref[0:8, :] = acc[r1 - r0 - 8:, :]
            y = _silu(y)
            if l2norm:
                scale = jnp.where(j < n_k_blocks, jnp.float32(q_scale), jnp.float32(1.0))
                for h in range(tn // HEAD_DIM):
                    cols = slice(h * HEAD_DIM, (h + 1) * HEAD_DIM)
                    blk = y[:, cols]
                    ss = jnp.sum(blk * blk, axis=-1, keepdims=True)
                    o_ref[rows, cols] = (blk * lax.rsqrt(ss + EPS) * scale).astype(o_ref.dtype)
            else:
                o_ref[rows, :] = y.astype(o_ref.dtype)
        return fn

    return [(j < 2 * n_k_blocks, conv(True)),
            ((j >= 2 * n_k_blocks) & (j < n_conv_blocks), conv(False)),
            (j >= n_conv_blocks, _store_cast(o_ref))]


def _swiglu_up_kernel(a_ref, wg_ref, wu_ref, o_ref, wg_bf, wu_bf):
    @pl.when(pl.program_id(1) == 0)
    def _():
        wg_bf[...] = wg_ref[...].astype(BF16)
        wu_bf[...] = wu_ref[...].astype(BF16)

    a = a_ref[...]
    g = _dot(a, wg_bf[...])
    u = _dot(a, wu_bf[...])
    o_ref[...] = (_silu(g) * u).astype(o_ref.dtype)


def swiglu_up(a, wg, wu, layer, *, tm, tn):
    M, K = a.shape
    N = wg.shape[2]
    return pl.pallas_call(
        _swiglu_up_kernel,
        grid=(N // tn, M // tm),
        in_specs=[pl.BlockSpec((tm, K), lambda j, i: (i, 0)),
                  _layer_w_spec(K, tn, layer),
                  _layer_w_spec(K, tn, layer)],
        out_specs=pl.BlockSpec((tm, tn), lambda j, i: (i, j)),
        out_shape=jax.ShapeDtypeStruct((M, N), BF16),
        scratch_shapes=[pltpu.VMEM((K, tn), BF16), pltpu.VMEM((K, tn), BF16)],
        compiler_params=_params("arbitrary", "arbitrary"),
        name="swiglu_up",
    )(a, wg, wu)


def _fox_attn_kernel(q_ref, k_ref, v_ref, og_ref, cq_ref, ck_ref, o_ref, *scratch, tq, nh):
    tk = tq
    hp = pl.program_id(1)
    qi = pl.program_id(2)
    heads = range(nh)
    s_scr, p_scr, m_ref, l_ref, alpha_ref, acc_ref = (scratch[i * nh:(i + 1) * nh] for i in range(6))
    cols = lambda e: slice(e * HEAD_DIM, (e + 1) * HEAD_DIM)
    lane = lax.broadcasted_iota(jnp.int32, cq_ref.shape, 1)
    cq_all = cq_ref[...]
    cq = [jnp.sum(jnp.where(lane == nh * hp + e, cq_all, 0.0), axis=-1, keepdims=True) * LOG2E for e in heads]

    def ck(e, kb):
        return ck_ref[e, pl.ds(kb, 1), :] * LOG2E

    def scores(e, kb):
        return _dot_nt(q_ref[:, cols(e)], k_ref[pl.ds(pl.multiple_of(kb * tk, tk), tk), cols(e)])

    def pv_dot(e, kb, slot):
        return _dot(p_scr[e][slot], v_ref[pl.ds(pl.multiple_of(kb * tk, tk), tk), cols(e)])

    def fold(e, pv):
        acc_ref[e][...] = alpha_ref[e][...] * acc_ref[e][...] + pv

    def softmax(e, t, slot, between):
        m_prev = m_ref[e][...]
        m_new = jnp.maximum(m_prev, jnp.max(t, axis=-1, keepdims=True) + cq[e])
        between()
        alpha = jnp.exp2(m_prev - m_new)
        p = jnp.exp2(t + (cq[e] - m_new))
        l_ref[e][...] = alpha * l_ref[e][...] + jnp.sum(p, axis=-1, keepdims=True)
        alpha_ref[e][...] = alpha
        p_scr[e][slot] = p.astype(p_scr[e].dtype)
        m_ref[e][...] = m_new

    for e in heads:
        m_ref[e][...] = jnp.full(m_ref[e].shape, -jnp.inf, F32)
        l_ref[e][...] = jnp.zeros(l_ref[e].shape, F32)
        alpha_ref[e][...] = jnp.zeros(alpha_ref[e].shape, F32)
        acc_ref[e][...] = jnp.zeros(acc_ref[e].shape, F32)
        p_scr[e][1] = jnp.zeros(p_scr[e].shape[1:], p_scr[e].dtype)
        s_scr[e][0] = scores(e, 0)

    def body(k, carry):
        cur = k % 2
        pv = [pv_dot(e, jnp.maximum(k - 1, 0), 1 - cur) for e in heads]
        s_next = [scores(e, k + 1) for e in heads]
        for e in heads:
            t = s_scr[e][cur] - ck(e, k)

            def between(e=e):
                fold(e, pv[e])
                s_scr[e][1 - cur] = s_next[e]

            softmax(e, t, cur, between)
        return carry

    lax.fori_loop(0, qi, body, 0)

    cur = qi % 2
    pv = [pv_dot(e, jnp.maximum(qi - 1, 0), 1 - cur) for e in heads]
    causal = (lax.broadcasted_iota(jnp.int32, (tq, tk), 0) >= lax.broadcasted_iota(jnp.int32, (tq, tk), 1))
    last_pv = []
    for e in heads:
        t = jnp.where(causal, s_scr[e][cur] - ck(e, qi), -jnp.inf)
        softmax(e, t, cur, lambda e=e: fold(e, pv[e]))
        last_pv.append(pv_dot(e, qi, cur))
    for e in heads:
        fold(e, last_pv[e])
        o = acc_ref[e][...] / l_ref[e][...]
        o_ref[:, cols(e)] = (o * _sigmoid(og_ref[:, cols(e)].astype(F32))).astype(o_ref.dtype)


def fox_attention(proj, cum, cum_rows, *, batch, heads, tq=512, nh=2):
    T = proj.shape[0]
    S = T // batch
    nq = S // tq
    H = heads
    nh = min(nh, H)
    G = H // nh
    w = nh * HEAD_DIM
    kern = functools.partial(_fox_attn_kernel, tq=tq, nh=nh)
    return pl.pallas_call(
        kern,
        grid=(batch, G, nq),
        in_specs=[
            pl.BlockSpec((tq, w), lambda b, g, qi: (b * nq + qi, g)),
            pl.BlockSpec((S, w), lambda b, g, qi: (b, G + g)),
            pl.BlockSpec((S, w), lambda b, g, qi: (b, 2 * G + g)),
            pl.BlockSpec((tq, w), lambda b, g, qi: (b * nq + qi, 3 * G + g)),
            pl.BlockSpec((tq, LANES), lambda b, g, qi: (b * nq + qi, 0)),
            pl.BlockSpec((nh, nq, tq), lambda b, g, qi: (b * G + g, 0, 0)),
        ],
        out_specs=pl.BlockSpec((tq, w), lambda b, g, qi: (b * nq + qi, g)),
        out_shape=jax.ShapeDtypeStruct((T, H * HEAD_DIM), BF16),
        scratch_shapes=[*[pltpu.VMEM((2, tq, tq), F32)] * nh, *[pltpu.VMEM((2, tq, tq), BF16)] * nh,
                        *[pltpu.VMEM((tq, 1), F32)] * (3 * nh), *[pltpu.VMEM((tq, HEAD_DIM), F32)] * nh],
        compiler_params=_params("parallel", "parallel", "arbitrary"),
        name="fox_attention",
    )(proj, proj, proj, proj, cum, cum_rows)


def _gdn_kernel(q_ref, k_ref, v_ref, z_ref, small_ref, grow_ref, gain_ref,
                o_ref, cols_ref, u_ref, w_ref, a_ref, o_scr, *, hv, nk, group):
    C = GDN_CHUNK
    S = q_ref.shape[0]
    n_chunks = S // C
    nv = 2 * nk
    hp = pl.program_id(1)
    cols = lambda i: slice(i * HEAD_DIM, (i + 1) * HEAD_DIM)

    cols_ref[...] = pltpu.roll(small_ref[...], (LANES - nv * hp) % LANES, 1)

    ri = lax.broadcasted_iota(jnp.int32, (C, C), 0)
    ci = lax.broadcasted_iota(jnp.int32, (C, C), 1)
    incl = ri >= ci
    strict = ri > ci
    eye = (ri == ci).astype(F32)
    steps = int(math.log2(C)) - 1

    def prep_stages(gi):
        chunk_rows = [pl.ds(pl.multiple_of((gi * group + t) * C, C), C) for t in range(group)]
        kf, qkk = {}, {}
        for t, rows in enumerate(chunk_rows):
            for kh in range(nk):
                kc = k_ref[rows, cols(kh)]
                qc = q_ref[rows, cols(kh)]
                kf[t, kh] = kc.astype(F32)
                qkk[t, kh] = _dot_nt(jnp.concatenate([kc, qc], axis=0), kc)
        yield
        chains = [(t, h) for t in range(group) for h in range(nv)]
        beta, g, x, lb = {}, {}, {}, {}
        for t, h in chains:
            rows = chunk_rows[t]
            g[t, h] = cols_ref[rows, hv + h:hv + h + 1]
            beta[t, h] = cols_ref[rows, h:h + 1]
            grow = grow_ref[h, pl.ds(gi * group + t, 1), :]
            decay = jnp.exp(jnp.where(incl, g[t, h] - grow, -jnp.inf))
            lmat = jnp.where(strict, beta[t, h] * qkk[t, h // 2][:C] * decay, 0.0)
            a_ref[h, rows, :] = jnp.where(incl, qkk[t, h // 2][C:] * decay, 0.0)
            x[t, h] = eye - lmat
            lb[t, h] = lmat.astype(BF16)
        p = {ch: _dot(lb[ch], lb[ch]) for ch in chains}
        yield
        for s in range(steps):
            if s < steps - 1:
                px = {ch: _dot(jnp.concatenate([p[ch], x[ch]], axis=0).astype(BF16), p[ch].astype(BF16))
                      for ch in chains}
                for ch in chains:
                    p[ch], x[ch] = px[ch][:C], x[ch] + px[ch][C:]
            else:
                xp = {ch: _dot(x[ch].astype(BF16), p[ch].astype(BF16)) for ch in chains}
                for ch in chains:
                    x[ch] = x[ch] + xp[ch]
            yield
        sol = {}
        for t, h in chains:
            vf = v_ref[chunk_rows[t], cols(h)].astype(F32)
            rhs = jnp.concatenate([vf * beta[t, h], kf[t, h // 2] * (beta[t, h] * jnp.exp(g[t, h]))], axis=1)
            sol[t, h] = _dot(x[t, h].astype(BF16), rhs.astype(BF16))
        for t, h in chains:
            u_ref[h, chunk_rows[t], :] = sol[t, h][:, :HEAD_DIM]
            w_ref[h, chunk_rows[t], :] = sol[t, h][:, HEAD_DIM:]
        yield

    def rec_stages(gi, states):
        for t in range(group):
            c = gi * group + t
            rows = pl.ds(pl.multiple_of(c * C, C), C)
            last = pl.ds(c * C + C - 1, 1)
            kf = [k_ref[rows, cols(kh)].astype(F32) for kh in range(nk)]
            qf = [q_ref[rows, cols(kh)].astype(F32) for kh in range(nk)]
            g = [cols_ref[rows, hv + h:hv + h + 1] for h in range(nv)]
            g_last = [cols_ref[last, hv + h:hv + h + 1] for h in range(nv)]
            ws_qs = []
            for h in range(nv):
                wq = jnp.concatenate([w_ref[h, rows, :], qf[h // 2] * jnp.exp(g[h])], axis=0)
                ws_qs.append(_dot(wq.astype(BF16), states[h].astype(BF16)))
            yield
            vb = [(u_ref[h, rows, :] - ws_qs[h][:C]).astype(BF16) for h in range(nv)]
            av = [_dot(a_ref[h, rows, :].astype(BF16), vb[h]) for h in range(nv)]
            kv = [_dot_tn((kf[h // 2] * jnp.exp(g_last[h] - g[h])).astype(BF16), vb[h]) for h in range(nv)]
            for h in range(nv):
                o_scr[h, rows, :] = ws_qs[h][C:] + av[h]
                states[h] = states[h] * jnp.exp(g_last[h]) + kv[h]
            yield

    def run_interleaved(*gens):
        live = list(gens)
        while live:
            for gen in list(live):
                if next(gen, "done") == "done":
                    live.remove(gen)

    n_groups = n_chunks // group
    run_interleaved(prep_stages(0))

    def body(gi, states):
        states = list(states)
        run_interleaved(rec_stages(gi, states), prep_stages(gi + 1))
        return tuple(states)

    zero = jnp.zeros((HEAD_DIM, HEAD_DIM), F32)
    states = list(lax.fori_loop(0, n_groups - 1, body, (zero,) * nv))
    run_interleaved(rec_stages(n_groups - 1, states))

    for h in range(nv):
        o = _rms(o_scr[h], gain_ref[...])
        z = z_ref[:, cols(h)].astype(F32)
        o_ref[:, cols(h)] = (o * _silu(z)).astype(o_ref.dtype)


def gated_delta(proj, small, grow, o_gain, *, batch, k_heads, nk=2, group=4):
    T = proj.shape[0]
    S = T // batch
    HK = k_heads
    HV = 2 * HK
    C = GDN_CHUNK
    n_chunks = S // C
    nk = min(nk, HK)
    nv = 2 * nk
    NP = HK // nk
    group = min(group, n_chunks)
    kern = functools.partial(_gdn_kernel, hv=HV, nk=nk, group=group)
    kw = nk * HEAD_DIM
    vw = nv * HEAD_DIM
    return pl.pallas_call(
        kern,
        grid=(batch, NP),
        in_specs=[
            pl.BlockSpec((S, kw), lambda b, h: (b, h)),
            pl.BlockSpec((S, kw), lambda b, h: (b, NP + h)),
            pl.BlockSpec((S, vw), lambda b, h: (b, NP + h)),
            pl.BlockSpec((S, vw), lambda b, h: (b, 2 * NP + h)),
            pl.BlockSpec((S, LANES), lambda b, h: (b, 0)),
            pl.BlockSpec((nv, n_chunks, C), lambda b, h: (b * NP + h, 0, 0)),
            pl.BlockSpec((1, HEAD_DIM), lambda b, h: (0, 0)),
        ],
        out_specs=pl.BlockSpec((S, vw), lambda b, h: (b, h)),
        out_shape=jax.ShapeDtypeStruct((T, HV * HEAD_DIM), BF16),
        scratch_shapes=[pltpu.VMEM((S, LANES), F32),
                        pltpu.VMEM((nv, S, HEAD_DIM), F32), pltpu.VMEM((nv, S, HEAD_DIM), F32),
                        pltpu.VMEM((nv, S, C), F32), pltpu.VMEM((nv, S, HEAD_DIM), F32)],
        compiler_params=_params("parallel", "parallel"),
        name="gated_delta",
    )(proj, proj, proj, proj, small, grow, o_gain.reshape(1, HEAD_DIM))


def _row_copy(src_ref, dst_ref, src_row, dst_row, sem):
    return pltpu.make_async_copy(src_ref.at[pl.ds(src_row, 1), :], dst_ref.at[pl.ds(dst_row, 1), :], sem)


def _gather_rows_kernel(idx_ref, nrows_ref, src_ref, o_ref, buf_ref, sem, *, rows):
    t = pl.program_id(0)
    slot = t % 2
    live = lambda step: step * rows < nrows_ref[0]

    def issue(step, sl):
        base = step * rows

        def start(g, c):
            for u in range(DMA_ISSUE_UNROLL):
                r = g * DMA_ISSUE_UNROLL + u
                _row_copy(src_ref, buf_ref.at[sl], idx_ref[base + r], r, sem.at[sl]).start(priority=u % 2)
            return c

        lax.fori_loop(0, rows // DMA_ISSUE_UNROLL, start, 0)

    @pl.when(t == 0)
    def _():
        issue(0, 0)

    @pl.when((t + 1 < pl.num_programs(0)) & live(t + 1))
    def _():
        issue(t + 1, 1 - slot)

    @pl.when(live(t))
    def _():
        def wait(r, c):
            _row_copy(src_ref, buf_ref.at[slot], 0, r, sem.at[slot]).wait()
            return c

        lax.fori_loop(0, rows, wait, 0, unroll=8)
        o_ref[...] = buf_ref[slot].astype(o_ref.dtype)

    @pl.when(jnp.logical_not(live(t)))
    def _():
        o_ref[...] = jnp.zeros(o_ref.shape, o_ref.dtype)


def gather_rows(src, idx, n_rows, *, rows=256):
    P = idx.shape[0]
    D = src.shape[1]
    return pl.pallas_call(
        functools.partial(_gather_rows_kernel, rows=rows),
        grid_spec=pltpu.PrefetchScalarGridSpec(
            num_scalar_prefetch=2,
            grid=(P // rows,),
            in_specs=[pl.BlockSpec(memory_space=pl.ANY)],
            out_specs=pl.BlockSpec((rows, D), lambda t, idx, nr: (t, 0)),
            scratch_shapes=[pltpu.VMEM((2, rows, D), F32), pltpu.SemaphoreType.DMA((2,))]),
        out_shape=jax.ShapeDtypeStruct((P, D), BF16),
        compiler_params=_params("arbitrary"),
        name="moe_gather",
    )(idx, n_rows, src)


def _combine_kernel(pos_ref, y_ref, h_ref, small_ref, o_ref, buf_ref, sem, *, rows):
    t = pl.program_id(0)
    slot = t % 2

    def issue(step, sl):
        base = step * rows

        def start(g, c):
            for u in range(DMA_ISSUE_UNROLL // 2):
                r = g * (DMA_ISSUE_UNROLL // 2) + u
                for k in range(2):
                    _row_copy(y_ref, buf_ref.at[sl, k], pos_ref[k, base + r], r, sem.at[sl]).start(priority=k)
            return c

        lax.fori_loop(0, rows // (DMA_ISSUE_UNROLL // 2), start, 0)

    @pl.when(t == 0)
    def _():
        issue(0, 0)

    @pl.when(t + 1 < pl.num_programs(0))
    def _():
        issue(t + 1, 1 - slot)

    def wait(r, c):
        for k in range(2):
            _row_copy(y_ref, buf_ref.at[slot, k], 0, r, sem.at[slot]).wait()
        return c

    lax.fori_loop(0, rows, wait, 0, unroll=4)
    w1 = small_ref[:, 2:3]
    w2 = small_ref[:, 3:4]
    o_ref[...] = h_ref[...] + (w1 * buf_ref[slot, 0] + w2 * buf_ref[slot, 1])


def moe_combine(y, pos, h, small, *, rows=256):
    T, D = h.shape
    return pl.pallas_call(
        functools.partial(_combine_kernel, rows=rows),
        grid_spec=pltpu.PrefetchScalarGridSpec(
            num_scalar_prefetch=1,
            grid=(T // rows,),
            in_specs=[pl.BlockSpec(memory_space=pl.ANY),
                      pl.BlockSpec((rows, D), lambda t, pos: (t, 0)),
                      pl.BlockSpec((rows, LANES), lambda t, pos: (t, 0))],
            out_specs=pl.BlockSpec((rows, D), lambda t, pos: (t, 0)),
            scratch_shapes=[pltpu.VMEM((2, 2, rows, D), F32), pltpu.SemaphoreType.DMA((2,))]),
        out_shape=jax.ShapeDtypeStruct((T, D), F32),
        compiler_params=_params("arbitrary"),
        name="moe_combine",
    )(pos, y, h, small)


def _grouped_kernel(te_ref, nv_ref, nxt_ref, x_ref, *rest, n_w, layer, tn, finish):
    w_hbm = rest[:n_w]
    o_ref = rest[n_w]
    stage = rest[n_w + 1:2 * n_w + 1]
    wbf = rest[2 * n_w + 1:3 * n_w + 1]
    sem = rest[3 * n_w + 1]
    j = pl.program_id(0)
    t = pl.program_id(1)
    e = te_ref[t]
    first = (t == 0) | (e != te_ref[jnp.maximum(t - 1, 0)])
    valid = t < nv_ref[0]

    def copies(jj, ee):
        cols = pl.ds(pl.multiple_of(jj * tn, LANES), tn)
        return [pltpu.make_async_copy(w_hbm[i].at[layer, ee, :, cols], stage[i], sem.at[i])
                for i in range(n_w)]

    @pl.when((j == 0) & (t == 0))
    def _():
        for c in copies(j, e):
            c.start()

    @pl.when(valid & first)
    def _():
        for i, c in enumerate(copies(j, e)):
            c.wait()
            wbf[i][...] = stage[i][...].astype(BF16)
        nt = nxt_ref[t]

        @pl.when(nt >= 0)
        def _():
            for c in copies(j, te_ref[jnp.maximum(nt, 0)]):
                c.start()

        @pl.when((nt < 0) & (j + 1 < pl.num_programs(0)))
        def _():
            for c in copies(j + 1, te_ref[0]):
                c.start()

    @pl.when(valid)
    def _():
        x = x_ref[...]
        o_ref[...] = finish([_dot(x, w[...]) for w in wbf]).astype(o_ref.dtype)

    @pl.when(jnp.logical_not(valid))
    def _():
        o_ref[...] = jnp.zeros(o_ref.shape, o_ref.dtype)


def grouped_matmul(x, weights, layer, tile_expert, n_valid, next_tile, *, tm, tn, out_dtype, finish, name):
    P, K = x.shape
    N = weights[0].shape[3]
    n_w = len(weights)
    last = lambda t, nv: jnp.minimum(t, nv[0] - 1)
    return pl.pallas_call(
        functools.partial(_grouped_kernel, n_w=n_w, layer=layer, tn=tn, finish=finish),
        grid_spec=pltpu.PrefetchScalarGridSpec(
            num_scalar_prefetch=3,
            grid=(N // tn, P // tm),
            in_specs=[pl.BlockSpec((tm, K), lambda j, t, te, nv, nx: (last(t, nv), 0)),
                      *[pl.BlockSpec(memory_space=pl.ANY)] * n_w],
            out_specs=pl.BlockSpec((tm, tn), lambda j, t, te, nv, nx: (t, j)),
            scratch_shapes=[*[pltpu.VMEM((K, tn), F32)] * n_w, *[pltpu.VMEM((K, tn), BF16)] * n_w,
                            pltpu.SemaphoreType.DMA((n_w,))]),
        out_shape=jax.ShapeDtypeStruct((P, N), out_dtype),
        compiler_params=_params("arbitrary", "arbitrary"),
        name=name,
    )(tile_expert, n_valid, next_tile, x, *weights)


def _moe_plan(small, *, tm):
    T = small.shape[0]
    experts = small[:, 0:2].astype(jnp.int32)
    flat = experts.T.reshape(-1)
    onehot = (flat[:, None] == jnp.arange(N_EXPERTS)[None, :]).astype(jnp.int32)
    rank = jnp.take_along_axis(jnp.cumsum(onehot, axis=0) - onehot, flat[:, None], axis=1)[:, 0]
    counts = jnp.sum(onehot, axis=0)
    tiles_per = (counts + tm - 1) // tm
    tile_end = jnp.cumsum(tiles_per)
    tile_start = tile_end - tiles_per
    pos = tile_start[flat] * tm + rank
    n_tiles = (2 * T) // tm + N_EXPERTS
    n_valid = tile_end[-1]
    tile_ids = jnp.minimum(jnp.arange(n_tiles), n_valid - 1)
    tile_expert = jnp.sum((tile_ids[:, None] >= tile_end[None, :]).astype(jnp.int32), axis=1)
    token = jnp.tile(jnp.arange(T, dtype=jnp.int32), 2)
    row_token = jnp.zeros((n_tiles * tm,), jnp.int32).at[pos].set(token)
    next_tile = jnp.where(tile_end[tile_expert] < n_valid, tile_end[tile_expert], -1)
    return (pos.reshape(2, T).astype(jnp.int32), row_token, tile_expert.astype(jnp.int32),
            n_valid.reshape(1).astype(jnp.int32), next_tile.astype(jnp.int32))


def moe_ffn(h, hn, small, wg, wu, wd, layer, *, tm=512):
    tm = min(tm, h.shape[0])
    pos, row_token, tile_expert, n_valid, next_tile = _moe_plan(small, tm=tm)
    xs = gather_rows(hn, row_token, n_valid * tm, rows=min(256, tm))
    N = wg.shape[3]
    tn_up = N // 2 if (N // 2) % LANES == 0 else N
    mid = grouped_matmul(xs, (wg, wu), layer, tile_expert, n_valid, next_tile, tm=tm, tn=tn_up,
                         out_dtype=BF16, finish=lambda d: _silu(d[0]) * d[1], name="expert_up")
    y = grouped_matmul(mid, (wd,), layer, tile_expert, n_valid, next_tile, tm=tm,
                       tn=min(1024, wd.shape[3]), out_dtype=F32, finish=lambda d: d[0], name="expert_down")
    return moe_combine(y, pos, h, small, rows=min(256, tm))


def _fox_layer(h, batch, j, norm_mix, norm_ffn, w_in_t, b_f, q_gain, k_gain, w_out, w_gate, w_up,
               w_down):
    T, D = h.shape
    S = T // batch
    H = b_f.shape[0]
    width = H * HEAD_DIM
    hn, lf = rmsnorm_small(h, norm_mix, _pad_cols(w_in_t[j, 4 * width:, :].T), _aux_rows(b_f), _post_fox,
                           name="fox_norm")
    cum = seq_cumsum(lf, batch)
    tq = min(512, S)
    cum_rows = cum[:, :H].reshape(batch, S, H).transpose(0, 2, 1).reshape(batch * H, S // tq, tq)
    tn = min(1024, width)
    proj = matmul(hn, w_in_t, j, n_cols=4 * width, tm=min(1024, T), tn=tn, out_dtype=BF16, w_transposed=True,
                  variants=functools.partial(_var_fox_in, n_q_blocks=width // tn),
                  extra=(q_gain.reshape(1, HEAD_DIM), k_gain.reshape(1, HEAD_DIM)),
                  extra_specs=(pl.BlockSpec((1, HEAD_DIM), lambda j, i: (0, 0)),) * 2,
                  name="fox_in_proj")
    o = fox_attention(proj, cum, cum_rows, batch=batch, heads=H, tq=tq)
    h = matmul_residual(o, w_out, j, h, tm=min(1024, T), tn=min(1024, D), name="fox_out_proj")
    hn2 = rmsnorm(h, norm_ffn)
    F = w_gate.shape[2]
    mid = swiglu_up(hn2, w_gate, w_up, j, tm=min(1024, T), tn=512 if F % 512 == 0 else F)
    return matmul_residual(mid, w_down, j, h, tm=min(512, T), tn=min(512, D), name="ffn_down")


def _gdn_layer(h, batch, j, norm_mix, norm_ffn, w_in_t, conv_w, a_log, dt_bias, o_gain, w_out,
               router, wg, wu, wd, *, seq_tile=1024):
    T, D = h.shape
    S = T // batch
    HV = a_log.shape[0]
    HK = HV // 2
    kw = HK * HEAD_DIM
    conv_ch = 4 * kw
    main = conv_ch + 2 * kw
    zeros = jnp.zeros((HV,), F32)
    aux = _aux_rows(jnp.concatenate([zeros, a_log]), jnp.concatenate([zeros, dt_bias]))
    hn, small = rmsnorm_small(h, norm_mix, _pad_cols(w_in_t[j, main:, :].T), aux,
                              functools.partial(_post_gdn, hv=HV), name="gdn_norm")
    small = chunk_cumsum(small, chunk=min(GDN_CHUNK, S), lo=HV, hi=2 * HV)
    C = GDN_CHUNK
    grow = small[:, HV:2 * HV].reshape(batch, S, HV).transpose(0, 2, 1).reshape(batch * HV, S // C, C)
    tn = min(1024, kw)
    tm = min(seq_tile, S)
    proj = matmul(hn, w_in_t, j, n_cols=main, tm=tm, tn=tn, out_dtype=BF16, w_transposed=True,
                  variants=functools.partial(_var_gdn_in, n_k_blocks=kw // tn,
                                             n_conv_blocks=conv_ch // tn, rows_per_seq=S // tm,
                                             q_scale=HEAD_DIM ** -0.5),
                  extra=(conv_w,),
                  extra_specs=(pl.BlockSpec((GDN_CONV, tn),
                                            lambda j, i: (0, jnp.minimum(j, conv_ch // tn - 1))),),
                  extra_scratch=(pltpu.VMEM((8 + tm, tn), F32),),
                  name="gdn_in_proj")
    o = gated_delta(proj, small, grow, o_gain, batch=batch, k_heads=HK)
    h = matmul_residual(o, w_out, j, h, tm=min(1024, T), tn=min(512, D), name="gdn_out_proj")
    hn2, route = rmsnorm_small(h, norm_ffn, _pad_cols(router), _aux_rows(), _post_moe,
                               hn_dtype=F32, name="moe_norm")
    return moe_ffn(h, hn2, route, wg, wu, wd, j)


def kernel(x, norm_mix, norm_ffn, fox_w_in, fox_b_f, fox_q_norm, fox_k_norm, fox_w_out,
           gdn_w_in, gdn_conv, gdn_a_log, gdn_dt_bias, gdn_o_norm, gdn_w_out,
           ffn_w_gate, ffn_w_up, ffn_w_down,
           moe_router, moe_w_gate, moe_w_up, moe_w_down):
    B, S, D = x.shape
    h = x.reshape(B * S, D)
    depth = norm_mix.shape[0]
    fox_w_in_t = jnp.swapaxes(fox_w_in, 1, 2)
    gdn_w_in_t = jnp.swapaxes(gdn_w_in, 1, 2)
    for i in range(depth):
        j = i // 2
        if i % 2 == 0:
            h = _fox_layer(h, B, j, norm_mix[i], norm_ffn[i], fox_w_in_t, fox_b_f[j], fox_q_norm[j],
                           fox_k_norm[j], fox_w_out, ffn_w_gate, ffn_w_up, ffn_w_down)
        else:
            h = _gdn_layer(h, B, j, norm_mix[i], norm_ffn[i], gdn_w_in_t, gdn_conv[j], gdn_a_log[j],
                           gdn_dt_bias[j], gdn_o_norm[j], gdn_w_out, moe_router[j],
                           moe_w_gate, moe_w_up, moe_w_down)
    return h.reshape(B, S, D)
```

```python
import functools
import math

import jax
import jax.numpy as jnp
from jax import lax
from jax.experimental import pallas as pl
from jax.experimental.pallas import tpu as pltpu

F32 = jnp.float32
BF16 = jnp.bfloat16
EPS = 1e-6
LANES = 128
HEAD_DIM = 128
GDN_CHUNK = 64
GDN_CONV = 4
N_EXPERTS = 8
VMEM_LIMIT_BYTES = 56 * 1024 * 1024
HIGHEST = lax.Precision.HIGHEST
LOG2E = math.log2(math.e)
DMA_ISSUE_UNROLL = 8


def _params(*sem):
    return pltpu.CompilerParams(dimension_semantics=sem, vmem_limit_bytes=VMEM_LIMIT_BYTES)


def _log1pexp_neg_abs(x):
    return jnp.log(1.0 + jnp.exp(-jnp.abs(x)))


def _log_sigmoid(x):
    return jnp.minimum(x, 0.0) - _log1pexp_neg_abs(x)


def _softplus(x):
    return jnp.maximum(x, 0.0) + _log1pexp_neg_abs(x)


def _sigmoid(x):
    return 1.0 / (1.0 + jnp.exp(-x))


def _silu(x):
    return x * _sigmoid(x)


def _dot(a, b):
    return jnp.dot(a, b, preferred_element_type=F32)


def _dot_nt(a, b):
    return lax.dot_general(a, b, (((1,), (1,)), ((), ())), preferred_element_type=F32)


def _dot_tn(a, b):
    return lax.dot_general(a, b, (((0,), (0,)), ((), ())), preferred_element_type=F32)


def _rms(x, g):
    ms = jnp.mean(x * x, axis=-1, keepdims=True)
    return x * lax.rsqrt(ms + EPS) * g


def _norm_kernel(x_ref, g_ref, hn_ref):
    hn_ref[...] = _rms(x_ref[...], g_ref[...]).astype(hn_ref.dtype)


def _split_bf16(a):
    hi = a.astype(BF16)
    return hi, (a - hi.astype(F32)).astype(BF16)


def _norm_small_kernel(x_ref, g_ref, ws_hi_ref, ws_lo_ref, aux_ref, hn_ref, small_ref, *, post):
    y = _rms(x_ref[...], g_ref[...])
    hn_ref[...] = y.astype(hn_ref.dtype)
    y_hi, y_lo = _split_bf16(y)
    w_hi = ws_hi_ref[...]
    s = _dot(y_hi, w_hi) + (_dot(y_hi, ws_lo_ref[...]) + _dot(y_lo, w_hi))
    small_ref[...] = post(s, aux_ref[...])


def rmsnorm(x, g, *, tm=512):
    T, D = x.shape
    return pl.pallas_call(
        _norm_kernel,
        grid=(T // tm,),
        in_specs=[pl.BlockSpec((tm, D), lambda i: (i, 0)),
                  pl.BlockSpec((1, D), lambda i: (0, 0))],
        out_specs=pl.BlockSpec((tm, D), lambda i: (i, 0)),
        out_shape=jax.ShapeDtypeStruct((T, D), BF16),
        compiler_params=_params("parallel"),
        name="rmsnorm",
    )(x, g.reshape(1, D))


def rmsnorm_small(x, g, w_small, aux, post, *, hn_dtype=BF16, tm=512, name="rmsnorm_small"):
    T, D = x.shape
    w_hi, w_lo = _split_bf16(w_small)
    return pl.pallas_call(
        functools.partial(_norm_small_kernel, post=post),
        grid=(T // tm,),
        in_specs=[pl.BlockSpec((tm, D), lambda i: (i, 0)),
                  pl.BlockSpec((1, D), lambda i: (0, 0)),
                  pl.BlockSpec((D, LANES), lambda i: (0, 0)),
                  pl.BlockSpec((D, LANES), lambda i: (0, 0)),
                  pl.BlockSpec((8, LANES), lambda i: (0, 0))],
        out_specs=[pl.BlockSpec((tm, D), lambda i: (i, 0)),
                   pl.BlockSpec((tm, LANES), lambda i: (i, 0))],
        out_shape=[jax.ShapeDtypeStruct((T, D), hn_dtype),
                   jax.ShapeDtypeStruct((T, LANES), F32)],
        compiler_params=_params("parallel"),
        name=name,
    )(x, g.reshape(1, D), w_hi, w_lo, aux)


def _pad_cols(w, n=LANES):
    return jnp.pad(w, ((0, 0), (0, n - w.shape[1])))


def _aux_rows(*rows):
    out = [jnp.pad(r.astype(F32), (0, LANES - r.shape[0])) for r in rows]
    out += [jnp.zeros((LANES,), F32)] * (8 - len(out))
    return jnp.stack(out)


def _post_fox(s, aux):
    return _log_sigmoid(s + aux[0:1, :])


def _post_gdn(s, aux, *, hv):
    lane = lax.broadcasted_iota(jnp.int32, s.shape, 1)
    beta = _sigmoid(s)
    g = -jnp.exp(aux[0:1, :]) * _softplus(s + aux[1:2, :])
    return jnp.where(lane < hv, beta, g)


def _post_moe(s, aux):
    del aux
    lane = lax.broadcasted_iota(jnp.int32, s.shape, 1).astype(F32)
    neg = jnp.float32(-jnp.inf)
    logits = jnp.where(lane < N_EXPERTS, s, neg)
    v1 = jnp.max(logits, axis=-1, keepdims=True)
    i1 = jnp.min(jnp.where(logits == v1, lane, float(LANES)), axis=-1, keepdims=True)
    rest = jnp.where(lane == i1, neg, logits)
    v2 = jnp.max(rest, axis=-1, keepdims=True)
    i2 = jnp.min(jnp.where(rest == v2, lane, float(LANES)), axis=-1, keepdims=True)
    e2 = jnp.exp(v2 - v1)
    denom = 1.0 + e2
    out = jnp.where(lane == 0.0, i1, 0.0)
    out = jnp.where(lane == 1.0, i2, out)
    out = jnp.where(lane == 2.0, 1.0 / denom, out)
    out = jnp.where(lane == 3.0, e2 / denom, out)
    return out


def _cumsum_kernel(x_ref, o_ref, *, blk):
    S = x_ref.shape[0]
    r = lax.broadcasted_iota(jnp.int32, (blk, blk), 0)
    c = lax.broadcasted_iota(jnp.int32, (blk, blk), 1)
    tri = (r >= c).astype(F32)
    carry = jnp.zeros((1, LANES), F32)
    for i in range(S // blk):
        cs = jnp.dot(tri, x_ref[i * blk:(i + 1) * blk, :], preferred_element_type=F32,
                     precision=HIGHEST) + carry
        o_ref[i * blk:(i + 1) * blk, :] = cs
        carry = cs[blk - 1:blk, :]


def seq_cumsum(x, batch, *, blk=128):
    T = x.shape[0]
    S = T // batch
    blk = min(blk, S)
    return pl.pallas_call(
        functools.partial(_cumsum_kernel, blk=blk),
        grid=(batch,),
        in_specs=[pl.BlockSpec((S, LANES), lambda b: (b, 0))],
        out_specs=pl.BlockSpec((S, LANES), lambda b: (b, 0)),
        out_shape=jax.ShapeDtypeStruct((T, LANES), F32),
        compiler_params=_params("parallel"),
        name="seq_cumsum",
    )(x)


def _chunk_cumsum_kernel(x_ref, o_ref, *, chunk, lo, hi):
    R = x_ref.shape[0]
    r = lax.broadcasted_iota(jnp.int32, (chunk, chunk), 0)
    c = lax.broadcasted_iota(jnp.int32, (chunk, chunk), 1)
    tri = (r >= c).astype(F32)
    lane = lax.broadcasted_iota(jnp.int32, (chunk, LANES), 1)
    sel = (lane >= lo) & (lane < hi)
    for i in range(R // chunk):
        x = x_ref[i * chunk:(i + 1) * chunk, :]
        cs = jnp.dot(tri, x, preferred_element_type=F32, precision=HIGHEST)
        o_ref[i * chunk:(i + 1) * chunk, :] = jnp.where(sel, cs, x)


def chunk_cumsum(x, *, chunk, lo, hi, tm=512):
    T = x.shape[0]
    tm = min(tm, T)
    return pl.pallas_call(
        functools.partial(_chunk_cumsum_kernel, chunk=chunk, lo=lo, hi=hi),
        grid=(T // tm,),
        in_specs=[pl.BlockSpec((tm, LANES), lambda i: (i, 0))],
        out_specs=pl.BlockSpec((tm, LANES), lambda i: (i, 0)),
        out_shape=jax.ShapeDtypeStruct((T, LANES), F32),
        compiler_params=_params("parallel"),
        name="chunk_cumsum",
    )(x)


def _mm_kernel(a_ref, w_ref, *rest, n_extra, variants, n_sub, w_transposed):
    extra = rest[:n_extra]
    o_ref = rest[n_extra]
    scratch = rest[n_extra + 1:]
    wbf_ref = scratch[0]

    @pl.when(pl.program_id(1) == 0)
    def _():
        w = w_ref[...]
        wbf_ref[...] = (w.T if w_transposed else w).astype(BF16)

    rs = a_ref.shape[0] // n_sub
    for cond, fn in variants(pl.program_id(0), pl.program_id(1), extra, o_ref, scratch[1:]):
        @pl.when(cond)
        def _(fn=fn):
            pending = None
            for sub in range(n_sub):
                rows = slice(sub * rs, (sub + 1) * rs)
                acc = _dot(a_ref[rows, :], wbf_ref[...])
                if pending is not None:
                    fn(*pending)
                pending = (acc, rows)
            fn(*pending)


def _layer_w_spec(K, tn, layer):
    return pl.BlockSpec((None, K, tn), lambda j, i: (layer, 0, j))


def matmul(a, w, layer, *, n_cols, tm, tn, out_dtype, variants, extra=(), extra_specs=(),
           extra_scratch=(), n_sub=4, w_transposed=False, name="matmul"):
    M, K = a.shape
    grid = (n_cols // tn, M // tm)
    w_spec = (pl.BlockSpec((None, tn, K), lambda j, i: (layer, j, 0)) if w_transposed
              else _layer_w_spec(K, tn, layer))
    return pl.pallas_call(
        functools.partial(_mm_kernel, n_extra=len(extra), variants=variants, n_sub=n_sub,
                          w_transposed=w_transposed),
        grid=grid,
        in_specs=[pl.BlockSpec((tm, K), lambda j, i: (i, 0)),
                  w_spec,
                  *extra_specs],
        out_specs=pl.BlockSpec((tm, tn), lambda j, i: (i, j)),
        out_shape=jax.ShapeDtypeStruct((M, n_cols), out_dtype),
        scratch_shapes=[pltpu.VMEM((K, tn), BF16), *extra_scratch],
        compiler_params=_params("arbitrary", "arbitrary"),
        name=name,
    )(a, w, *extra)


def _var_residual(j, i, extra, o_ref, scratch):
    del j, i, scratch

    def fn(acc, rows):
        o_ref[rows, :] = extra[0][rows, :] + acc

    return [(True, fn)]


def matmul_residual(a, w, layer, res, *, tm, tn, name):
    N = w.shape[2]
    return matmul(a, w, layer, n_cols=N, tm=tm, tn=tn, out_dtype=F32, variants=_var_residual,
                  extra=(res,), extra_specs=(pl.BlockSpec((tm, tn), lambda j, i: (i, j)),),
                  name=name)


def _store_cast(o_ref):
    def fn(acc, rows):
        o_ref[rows, :] = acc.astype(o_ref.dtype)
    return fn


def _var_fox_in(j, i, extra, o_ref, scratch, *, n_q_blocks):
    del i, scratch
    qg_ref, kg_ref = extra
    tn = o_ref.shape[1]

    def normed(gain_ref, scale):
        def fn(acc, rows):
            for h in range(tn // HEAD_DIM):
                cols = slice(h * HEAD_DIM, (h + 1) * HEAD_DIM)
                o_ref[rows, cols] = (_rms(acc[:, cols], gain_ref[...]) * scale).astype(o_ref.dtype)
        return fn

    return [(j < n_q_blocks, normed(qg_ref, HEAD_DIM ** -0.5 * LOG2E)),
            ((j >= n_q_blocks) & (j < 2 * n_q_blocks), normed(kg_ref, 1.0)),
            (j >= 2 * n_q_blocks, _store_cast(o_ref))]


def _var_gdn_in(j, i, extra, o_ref, scratch, *, n_k_blocks, n_conv_blocks, rows_per_seq, q_scale):
    conv_ref, = extra
    buf_ref, = scratch
    tm, tn = o_ref.shape

    @pl.when(i % rows_per_seq == 0)
    def _():
        buf_ref[0:8, :] = jnp.zeros((8, tn), F32)

    def conv(l2norm):
        def fn(acc, rows):
            r0, r1 = rows.start, rows.stop
            buf_ref[8 + r0:8 + r1, :] = acc
            y = acc * conv_ref[GDN_CONV - 1:GDN_CONV, :]
            for tap in range(GDN_CONV - 1):
                shift = GDN_CONV - 1 - tap
                y = y + buf_ref[8 + r0 - shift:8 + r1 - shift, :] * conv_ref[tap:tap + 1, :]
            if r1 == tm:
                buf_ref[0:8, :] = acc[r1 - r0 - 8:, :]
            y = _silu(y)
            if l2norm:
                scale = jnp.where(j < n_k_blocks, jnp.float32(q_scale), jnp.float32(1.0))
                for h in range(tn // HEAD_DIM):
                    cols = slice(h * HEAD_DIM, (h + 1) * HEAD_DIM)
                    blk = y[:, cols]
                    ss = jnp.sum(blk * blk, axis=-1, keepdims=True)
                    o_ref[rows, cols] = (blk * lax.rsqrt(ss + EPS) * scale).astype(o_ref.dtype)
            else:
                o_ref[rows, :] = y.astype(o_ref.dtype)
        return fn

    return [(j < 2 * n_k_blocks, conv(True)),
            ((j >= 2 * n_k_blocks) & (j < n_conv_blocks), conv(False)),
            (j >= n_conv_blocks, _store_cast(o_ref))]


def _swiglu_up_kernel(a_ref, wg_ref, wu_ref, o_ref, wg_bf, wu_bf):
    @pl.when(pl.program_id(1) == 0)
    def _():
        wg_bf[...] = wg_ref[...].astype(BF16)
        wu_bf[...] = wu_ref[...].astype(BF16)

    a = a_ref[...]
    g = _dot(a, wg_bf[...])
    u = _dot(a, wu_bf[...])
    o_ref[...] = (_silu(g) * u).astype(o_ref.dtype)


def swiglu_up(a, wg, wu, layer, *, tm, tn):
    M, K = a.shape
    N = wg.shape[2]
    return pl.pallas_call(
        _swiglu_up_kernel,
        grid=(N // tn, M // tm),
        in_specs=[pl.BlockSpec((tm, K), lambda j, i: (i, 0)),
                  _layer_w_spec(K, tn, layer),
                  _layer_w_spec(K, tn, layer)],
        out_specs=pl.BlockSpec((tm, tn), lambda j, i: (i, j)),
        out_shape=jax.ShapeDtypeStruct((M, N), BF16),
        scratch_shapes=[pltpu.VMEM((K, tn), BF16), pltpu.VMEM((K, tn), BF16)],
        compiler_params=_params("arbitrary", "arbitrary"),
        name="swiglu_up",
    )(a, wg, wu)


def _fox_attn_kernel(q_ref, k_ref, v_ref, og_ref, cq_ref, ck_ref, o_ref, *scratch, tq, nh):
    tk = tq
    hp = pl.program_id(1)
    qi = pl.program_id(2)
    heads = range(nh)
    s_scr, p_scr, m_ref, l_ref, alpha_ref, acc_ref = (scratch[i * nh:(i + 1) * nh] for i in range(6))
    cols = lambda e: slice(e * HEAD_DIM, (e + 1) * HEAD_DIM)
    lane = lax.broadcasted_iota(jnp.int32, cq_ref.shape, 1)
    cq_all = cq_ref[...]
    cq = [jnp.sum(jnp.where(lane == nh * hp + e, cq_all, 0.0), axis=-1, keepdims=True) * LOG2E for e in heads]

    def ck(e, kb):
        return ck_ref[e, pl.ds(kb, 1), :] * LOG2E

    def scores(e, kb):
        return _dot_nt(q_ref[:, cols(e)], k_ref[pl.ds(pl.multiple_of(kb * tk, tk), tk), cols(e)])

    def pv_dot(e, kb, slot):
        return _dot(p_scr[e][slot], v_ref[pl.ds(pl.multiple_of(kb * tk, tk), tk), cols(e)])

    def fold(e, pv):
        acc_ref[e][...] = alpha_ref[e][...] * acc_ref[e][...] + pv

    def softmax(e, t, slot, between):
        m_prev = m_ref[e][...]
        m_new = jnp.maximum(m_prev, jnp.max(t, axis=-1, keepdims=True) + cq[e])
        between()
        alpha = jnp.exp2(m_prev - m_new)
        p = jnp.exp2(t + (cq[e] - m_new))
        l_ref[e][...] = alpha * l_ref[e][...] + jnp.sum(p, axis=-1, keepdims=True)
        alpha_ref[e][...] = alpha
        p_scr[e][slot] = p.astype(p_scr[e].dtype)
        m_ref[e][...] = m_new

    for e in heads:
        m_ref[e][...] = jnp.full(m_ref[e].shape, -jnp.inf, F32)
        l_ref[e][...] = jnp.zeros(l_ref[e].shape, F32)
        alpha_ref[e][...] = jnp.zeros(alpha_ref[e].shape, F32)
        acc_ref[e][...] = jnp.zeros(acc_ref[e].shape, F32)
        p_scr[e][1] = jnp.zeros(p_scr[e].shape[1:], p_scr[e].dtype)
        s_scr[e][0] = scores(e, 0)

    def body(k, carry):
        cur = k % 2
        pv = [pv_dot(e, jnp.maximum(k - 1, 0), 1 - cur) for e in heads]
        s_next = [scores(e, k + 1) for e in heads]
        for e in heads:
            t = s_scr[e][cur] - ck(e, k)

            def between(e=e):
                fold(e, pv[e])
                s_scr[e][1 - cur] = s_next[e]

            softmax(e, t, cur, between)
        return carry

    lax.fori_loop(0, qi, body, 0)

    cur = qi % 2
    pv = [pv_dot(e, jnp.maximum(qi - 1, 0), 1 - cur) for e in heads]
    causal = (lax.broadcasted_iota(jnp.int32, (tq, tk), 0) >= lax.broadcasted_iota(jnp.int32, (tq, tk), 1))
    last_pv = []
    for e in heads:
        t = jnp.where(causal, s_scr[e][cur] - ck(e, qi), -jnp.inf)
        softmax(e, t, cur, lambda e=e: fold(e, pv[e]))
        last_pv.append(pv_dot(e, qi, cur))
    for e in heads:
        fold(e, last_pv[e])
        o = acc_ref[e][...] / l_ref[e][...]
        o_ref[:, cols(e)] = (o * _sigmoid(og_ref[:, cols(e)].astype(F32))).astype(o_ref.dtype)


def fox_attention(proj, cum, cum_rows, *, batch, heads, tq=512, nh=4):
    T = proj.shape[0]
    S = T // batch
    nq = S // tq
    H = heads
    nh = min(nh, H)
    G = H // nh
    w = nh * HEAD_DIM
    kern = functools.partial(_fox_attn_kernel, tq=tq, nh=nh)
    return pl.pallas_call(
        kern,
        grid=(batch, G, nq),
        in_specs=[
            pl.BlockSpec((tq, w), lambda b, g, qi: (b * nq + qi, g)),
            pl.BlockSpec((S, w), lambda b, g, qi: (b, G + g)),
            pl.BlockSpec((S, w), lambda b, g, qi: (b, 2 * G + g)),
            pl.BlockSpec((tq, w), lambda b, g, qi: (b * nq + qi, 3 * G + g)),
            pl.BlockSpec((tq, LANES), lambda b, g, qi: (b * nq + qi, 0)),
            pl.BlockSpec((nh, nq, tq), lambda b, g, qi: (b * G + g, 0, 0)),
        ],
        out_specs=pl.BlockSpec((tq, w), lambda b, g, qi: (b * nq + qi, g)),
        out_shape=jax.ShapeDtypeStruct((T, H * HEAD_DIM), BF16),
        scratch_shapes=[*[pltpu.VMEM((2, tq, tq), F32)] * nh, *[pltpu.VMEM((2, tq, tq), BF16)] * nh,
                        *[pltpu.VMEM((tq, 1), F32)] * (3 * nh), *[pltpu.VMEM((tq, HEAD_DIM), F32)] * nh],
        compiler_params=_params("parallel", "parallel", "arbitrary"),
        name="fox_attention",
    )(proj, proj, proj, proj, cum, cum_rows)


def _gdn_kernel(q_ref, k_ref, v_ref, z_ref, small_ref, grow_ref, gain_ref,
                o_ref, cols_ref, u_ref, w_ref, a_ref, o_scr, *, hv, nk, group):
    C = GDN_CHUNK
    S = q_ref.shape[0]
    n_chunks = S // C
    nv = 2 * nk
    hp = pl.program_id(1)
    cols = lambda i: slice(i * HEAD_DIM, (i + 1) * HEAD_DIM)

    cols_ref[...] = pltpu.roll(small_ref[...], (LANES - nv * hp) % LANES, 1)

    ri = lax.broadcasted_iota(jnp.int32, (C, C), 0)
    ci = lax.broadcasted_iota(jnp.int32, (C, C), 1)
    incl = ri >= ci
    strict = ri > ci
    eye = (ri == ci).astype(F32)
    steps = int(math.log2(C)) - 1

    def prep_stages(gi):
        chunk_rows = [pl.ds(pl.multiple_of((gi * group + t) * C, C), C) for t in range(group)]
        kf, qkk = {}, {}
        for t, rows in enumerate(chunk_rows):
            for kh in range(nk):
                kc = k_ref[rows, cols(kh)]
                qc = q_ref[rows, cols(kh)]
                kf[t, kh] = kc.astype(F32)
                qkk[t, kh] = _dot_nt(jnp.concatenate([kc, qc], axis=0), kc)
        yield
        chains = [(t, h) for t in range(group) for h in range(nv)]
        beta, g, x, lb = {}, {}, {}, {}
        for t, h in chains:
            rows = chunk_rows[t]
            g[t, h] = cols_ref[rows, hv + h:hv + h + 1]
            beta[t, h] = cols_ref[rows, h:h + 1]
            grow = grow_ref[h, pl.ds(gi * group + t, 1), :]
            decay = jnp.exp(jnp.where(incl, g[t, h] - grow, -jnp.inf))
            lmat = jnp.where(strict, beta[t, h] * qkk[t, h // 2][:C] * decay, 0.0)
            a_ref[h, rows, :] = jnp.where(incl, qkk[t, h // 2][C:] * decay, 0.0)
            x[t, h] = eye - lmat
            lb[t, h] = lmat.astype(BF16)
        p = {ch: _dot(lb[ch], lb[ch]) for ch in chains}
        yield
        for s in range(steps):
            if s < steps - 1:
                px = {ch: _dot(jnp.concatenate([p[ch], x[ch]], axis=0).astype(BF16), p[ch].astype(BF16))
                      for ch in chains}
                for ch in chains:
                    p[ch], x[ch] = px[ch][:C], x[ch] + px[ch][C:]
            else:
                xp = {ch: _dot(x[ch].astype(BF16), p[ch].astype(BF16)) for ch in chains}
                for ch in chains:
                    x[ch] = x[ch] + xp[ch]
            yield
        sol = {}
        for t, h in chains:
            vf = v_ref[chunk_rows[t], cols(h)].astype(F32)
            rhs = jnp.concatenate([vf * beta[t, h], kf[t, h // 2] * (beta[t, h] * jnp.exp(g[t, h]))], axis=1)
            sol[t, h] = _dot(x[t, h].astype(BF16), rhs.astype(BF16))
        for t, h in chains:
            u_ref[h, chunk_rows[t], :] = sol[t, h][:, :HEAD_DIM]
            w_ref[h, chunk_rows[t], :] = sol[t, h][:, HEAD_DIM:]
        yield

    def rec_stages(gi, states):
        for t in range(group):
            c = gi * group + t
            rows = pl.ds(pl.multiple_of(c * C, C), C)
            last = pl.ds(c * C + C - 1, 1)
            kf = [k_ref[rows, cols(kh)].astype(F32) for kh in range(nk)]
            qf = [q_ref[rows, cols(kh)].astype(F32) for kh in range(nk)]
            g = [cols_ref[rows, hv + h:hv + h + 1] for h in range(nv)]
            g_last = [cols_ref[last, hv + h:hv + h + 1] for h in range(nv)]
            ws_qs = []
            for h in range(nv):
                wq = jnp.concatenate([w_ref[h, rows, :], qf[h // 2] * jnp.exp(g[h])], axis=0)
                ws_qs.append(_dot(wq.astype(BF16), states[h].astype(BF16)))
            yield
            vb = [(u_ref[h, rows, :] - ws_qs[h][:C]).astype(BF16) for h in range(nv)]
            av = [_dot(a_ref[h, rows, :].astype(BF16), vb[h]) for h in range(nv)]
            kv = [_dot_tn((kf[h // 2] * jnp.exp(g_last[h] - g[h])).astype(BF16), vb[h]) for h in range(nv)]
            for h in range(nv):
                o_scr[h, rows, :] = ws_qs[h][C:] + av[h]
                states[h] = states[h] * jnp.exp(g_last[h]) + kv[h]
            yield

    def run_interleaved(*gens):
        live = list(gens)
        while live:
            for gen in list(live):
                if next(gen, "done") == "done":
                    live.remove(gen)

    n_groups = n_chunks // group
    run_interleaved(prep_stages(0))

    def body(gi, states):
        states = list(states)
        run_interleaved(rec_stages(gi, states), prep_stages(gi + 1))
        return tuple(states)

    zero = jnp.zeros((HEAD_DIM, HEAD_DIM), F32)
    states = list(lax.fori_loop(0, n_groups - 1, body, (zero,) * nv))
    run_interleaved(rec_stages(n_groups - 1, states))

    for h in range(nv):
        o = _rms(o_scr[h], gain_ref[...])
        z = z_ref[:, cols(h)].astype(F32)
        o_ref[:, cols(h)] = (o * _silu(z)).astype(o_ref.dtype)


def gated_delta(proj, small, grow, o_gain, *, batch, k_heads, nk=2, group=4):
    T = proj.shape[0]
    S = T // batch
    HK = k_heads
    HV = 2 * HK
    C = GDN_CHUNK
    n_chunks = S // C
    nk = min(nk, HK)
    nv = 2 * nk
    NP = HK // nk
    group = min(group, n_chunks)
    kern = functools.partial(_gdn_kernel, hv=HV, nk=nk, group=group)
    kw = nk * HEAD_DIM
    vw = nv * HEAD_DIM
    return pl.pallas_call(
        kern,
        grid=(batch, NP),
        in_specs=[
            pl.BlockSpec((S, kw), lambda b, h: (b, h)),
            pl.BlockSpec((S, kw), lambda b, h: (b, NP + h)),
            pl.BlockSpec((S, vw), lambda b, h: (b, NP + h)),
            pl.BlockSpec((S, vw), lambda b, h: (b, 2 * NP + h)),
            pl.BlockSpec((S, LANES), lambda b, h: (b, 0)),
            pl.BlockSpec((nv, n_chunks, C), lambda b, h: (b * NP + h, 0, 0)),
            pl.BlockSpec((1, HEAD_DIM), lambda b, h: (0, 0)),
        ],
        out_specs=pl.BlockSpec((S, vw), lambda b, h: (b, h)),
        out_shape=jax.ShapeDtypeStruct((T, HV * HEAD_DIM), BF16),
        scratch_shapes=[pltpu.VMEM((S, LANES), F32),
                        pltpu.VMEM((nv, S, HEAD_DIM), F32), pltpu.VMEM((nv, S, HEAD_DIM), F32),
                        pltpu.VMEM((nv, S, C), F32), pltpu.VMEM((nv, S, HEAD_DIM), F32)],
        compiler_params=_params("parallel", "parallel"),
        name="gated_delta",
    )(proj, proj, proj, proj, small, grow, o_gain.reshape(1, HEAD_DIM))


def _row_copy(src_ref, dst_ref, src_row, dst_row, sem):
    return pltpu.make_async_copy(src_ref.at[pl.ds(src_row, 1), :], dst_ref.at[pl.ds(dst_row, 1), :], sem)


def _gather_rows_kernel(idx_ref, nrows_ref, src_ref, o_ref, buf_ref, sem, *, rows):
    t = pl.program_id(0)
    slot = t % 2
    live = lambda step: step * rows < nrows_ref[0]

    def issue(step, sl):
        base = step * rows

        def start(g, c):
            for u in range(DMA_ISSUE_UNROLL):
                r = g * DMA_ISSUE_UNROLL + u
                _row_copy(src_ref, buf_ref.at[sl], idx_ref[base + r], r, sem.at[sl]).start(priority=u % 2)
            return c

        lax.fori_loop(0, rows // DMA_ISSUE_UNROLL, start, 0)

    @pl.when(t == 0)
    def _():
        issue(0, 0)

    @pl.when((t + 1 < pl.num_programs(0)) & live(t + 1))
    def _():
        issue(t + 1, 1 - slot)

    @pl.when(live(t))
    def _():
        def wait(r, c):
            _row_copy(src_ref, buf_ref.at[slot], 0, r, sem.at[slot]).wait()
            return c

        lax.fori_loop(0, rows, wait, 0, unroll=8)
        o_ref[...] = buf_ref[slot].astype(o_ref.dtype)

    @pl.when(jnp.logical_not(live(t)))
    def _():
        o_ref[...] = jnp.zeros(o_ref.shape, o_ref.dtype)


def gather_rows(src, idx, n_rows, *, rows=256):
    P = idx.shape[0]
    D = src.shape[1]
    return pl.pallas_call(
        functools.partial(_gather_rows_kernel, rows=rows),
        grid_spec=pltpu.PrefetchScalarGridSpec(
            num_scalar_prefetch=2,
            grid=(P // rows,),
            in_specs=[pl.BlockSpec(memory_space=pl.ANY)],
            out_specs=pl.BlockSpec((rows, D), lambda t, idx, nr: (t, 0)),
            scratch_shapes=[pltpu.VMEM((2, rows, D), F32), pltpu.SemaphoreType.DMA((2,))]),
        out_shape=jax.ShapeDtypeStruct((P, D), BF16),
        compiler_params=_params("arbitrary"),
        name="moe_gather",
    )(idx, n_rows, src)


def _combine_kernel(pos_ref, y_ref, h_ref, small_ref, o_ref, buf_ref, sem, *, rows):
    t = pl.program_id(0)
    slot = t % 2

    def issue(step, sl):
        base = step * rows

        def start(g, c):
            for u in range(DMA_ISSUE_UNROLL // 2):
                r = g * (DMA_ISSUE_UNROLL // 2) + u
                for k in range(2):
                    _row_copy(y_ref, buf_ref.at[sl, k], pos_ref[k, base + r], r, sem.at[sl]).start(priority=k)
            return c

        lax.fori_loop(0, rows // (DMA_ISSUE_UNROLL // 2), start, 0)

    @pl.when(t == 0)
    def _():
        issue(0, 0)

    @pl.when(t + 1 < pl.num_programs(0))
    def _():
        issue(t + 1, 1 - slot)

    def wait(r, c):
        for k in range(2):
            _row_copy(y_ref, buf_ref.at[slot, k], 0, r, sem.at[slot]).wait()
        return c

    lax.fori_loop(0, rows, wait, 0, unroll=4)
    w1 = small_ref[:, 2:3]
    w2 = small_ref[:, 3:4]
    o_ref[...] = h_ref[...] + (w1 * buf_ref[slot, 0] + w2 * buf_ref[slot, 1])


def moe_combine(y, pos, h, small, *, rows=256):
    T, D = h.shape
    return pl.pallas_call(
        functools.partial(_combine_kernel, rows=rows),
        grid_spec=pltpu.PrefetchScalarGridSpec(
            num_scalar_prefetch=1,
            grid=(T // rows,),
            in_specs=[pl.BlockSpec(memory_space=pl.ANY),
                      pl.BlockSpec((rows, D), lambda t, pos: (t, 0)),
                      pl.BlockSpec((rows, LANES), lambda t, pos: (t, 0))],
            out_specs=pl.BlockSpec((rows, D), lambda t, pos: (t, 0)),
            scratch_shapes=[pltpu.VMEM((2, 2, rows, D), F32), pltpu.SemaphoreType.DMA((2,))]),
        out_shape=jax.ShapeDtypeStruct((T, D), F32),
        compiler_params=_params("arbitrary"),
        name="moe_combine",
    )(pos, y, h, small)


def _grouped_kernel(te_ref, nv_ref, nxt_ref, x_ref, *rest, n_w, layer, tn, finish):
    w_hbm = rest[:n_w]
    o_ref = rest[n_w]
    stage = rest[n_w + 1:2 * n_w + 1]
    wbf = rest[2 * n_w + 1:3 * n_w + 1]
    sem = rest[3 * n_w + 1]
    j = pl.program_id(0)
    t = pl.program_id(1)
    e = te_ref[t]
    first = (t == 0) | (e != te_ref[jnp.maximum(t - 1, 0)])
    valid = t < nv_ref[0]

    def copies(jj, ee):
        cols = pl.ds(pl.multiple_of(jj * tn, LANES), tn)
        return [pltpu.make_async_copy(w_hbm[i].at[layer, ee, :, cols], stage[i], sem.at[i])
                for i in range(n_w)]

    @pl.when((j == 0) & (t == 0))
    def _():
        for c in copies(j, e):
            c.start()

    @pl.when(valid & first)
    def _():
        for i, c in enumerate(copies(j, e)):
            c.wait()
            wbf[i][...] = stage[i][...].astype(BF16)
        nt = nxt_ref[t]

        @pl.when(nt >= 0)
        def _():
            for c in copies(j, te_ref[jnp.maximum(nt, 0)]):
                c.start()

        @pl.when((nt < 0) & (j + 1 < pl.num_programs(0)))
        def _():
            for c in copies(j + 1, te_ref[0]):
                c.start()

    @pl.when(valid)
    def _():
        x = x_ref[...]
        o_ref[...] = finish([_dot(x, w[...]) for w in wbf]).astype(o_ref.dtype)

    @pl.when(jnp.logical_not(valid))
    def _():
        o_ref[...] = jnp.zeros(o_ref.shape, o_ref.dtype)


def grouped_matmul(x, weights, layer, tile_expert, n_valid, next_tile, *, tm, tn, out_dtype, finish, name):
    P, K = x.shape
    N = weights[0].shape[3]
    n_w = len(weights)
    last = lambda t, nv: jnp.minimum(t, nv[0] - 1)
    return pl.pallas_call(
        functools.partial(_grouped_kernel, n_w=n_w, layer=layer, tn=tn, finish=finish),
        grid_spec=pltpu.PrefetchScalarGridSpec(
            num_scalar_prefetch=3,
            grid=(N // tn, P // tm),
            in_specs=[pl.BlockSpec((tm, K), lambda j, t, te, nv, nx: (last(t, nv), 0)),
                      *[pl.BlockSpec(memory_space=pl.ANY)] * n_w],
            out_specs=pl.BlockSpec((tm, tn), lambda j, t, te, nv, nx: (t, j)),
            scratch_shapes=[*[pltpu.VMEM((K, tn), F32)] * n_w, *[pltpu.VMEM((K, tn), BF16)] * n_w,
                            pltpu.SemaphoreType.DMA((n_w,))]),
        out_shape=jax.ShapeDtypeStruct((P, N), out_dtype),
        compiler_params=_params("arbitrary", "arbitrary"),
        name=name,
    )(tile_expert, n_valid, next_tile, x, *weights)


def _moe_plan(small, *, tm):
    T = small.shape[0]
    experts = small[:, 0:2].astype(jnp.int32)
    flat = experts.T.reshape(-1)
    onehot = (flat[:, None] == jnp.arange(N_EXPERTS)[None, :]).astype(jnp.int32)
    rank = jnp.take_along_axis(jnp.cumsum(onehot, axis=0) - onehot, flat[:, None], axis=1)[:, 0]
    counts = jnp.sum(onehot, axis=0)
    tiles_per = (counts + tm - 1) // tm
    tile_end = jnp.cumsum(tiles_per)
    tile_start = tile_end - tiles_per
    pos = tile_start[flat] * tm + rank
    n_tiles = (2 * T) // tm + N_EXPERTS
    n_valid = tile_end[-1]
    tile_ids = jnp.minimum(jnp.arange(n_tiles), n_valid - 1)
    tile_expert = jnp.sum((tile_ids[:, None] >= tile_end[None, :]).astype(jnp.int32), axis=1)
    token = jnp.tile(jnp.arange(T, dtype=jnp.int32), 2)
    row_token = jnp.zeros((n_tiles * tm,), jnp.int32).at[pos].set(token)
    next_tile = jnp.where(tile_end[tile_expert] < n_valid, tile_end[tile_expert], -1)
    return (pos.reshape(2, T).astype(jnp.int32), row_token, tile_expert.astype(jnp.int32),
            n_valid.reshape(1).astype(jnp.int32), next_tile.astype(jnp.int32))


def moe_ffn(h, hn, small, wg, wu, wd, layer, *, tm=512):
    tm = min(tm, h.shape[0])
    pos, row_token, tile_expert, n_valid, next_tile = _moe_plan(small, tm=tm)
    xs = gather_rows(hn, row_token, n_valid * tm, rows=min(256, tm))
    N = wg.shape[3]
    tn_up = N // 2 if (N // 2) % LANES == 0 else N
    mid = grouped_matmul(xs, (wg, wu), layer, tile_expert, n_valid, next_tile, tm=tm, tn=tn_up,
                         out_dtype=BF16, finish=lambda d: _silu(d[0]) * d[1], name="expert_up")
    y = grouped_matmul(mid, (wd,), layer, tile_expert, n_valid, next_tile, tm=tm,
                       tn=min(1024, wd.shape[3]), out_dtype=F32, finish=lambda d: d[0], name="expert_down")
    return moe_combine(y, pos, h, small, rows=min(256, tm))


def _fox_layer(h, batch, j, norm_mix, norm_ffn, w_in_t, b_f, q_gain, k_gain, w_out, w_gate, w_up,
               w_down):
    T, D = h.shape
    S = T // batch
    H = b_f.shape[0]
    width = H * HEAD_DIM
    hn, lf = rmsnorm_small(h, norm_mix, _pad_cols(w_in_t[j, 4 * width:, :].T), _aux_rows(b_f), _post_fox,
                           name="fox_norm")
    cum = seq_cumsum(lf, batch)
    tq = min(512, S)
    cum_rows = cum[:, :H].reshape(batch, S, H).transpose(0, 2, 1).reshape(batch * H, S // tq, tq)
    tn = min(1024, width)
    proj = matmul(hn, w_in_t, j, n_cols=4 * width, tm=min(1024, T), tn=tn, out_dtype=BF16, w_transposed=True,
                  variants=functools.partial(_var_fox_in, n_q_blocks=width // tn),
                  extra=(q_gain.reshape(1, HEAD_DIM), k_gain.reshape(1, HEAD_DIM)),
                  extra_specs=(pl.BlockSpec((1, HEAD_DIM), lambda j, i: (0, 0)),) * 2,
                  name="fox_in_proj")
    o = fox_attention(proj, cum, cum_rows, batch=batch, heads=H, tq=tq)
    h = matmul_residual(o, w_out, j, h, tm=min(1024, T), tn=min(1024, D), name="fox_out_proj")
    hn2 = rmsnorm(h, norm_ffn)
    F = w_gate.shape[2]
    mid = swiglu_up(hn2, w_gate, w_up, j, tm=min(1024, T), tn=512 if F % 512 == 0 else F)
    return matmul_residual(mid, w_down, j, h, tm=min(512, T), tn=min(512, D), name="ffn_down")


def _gdn_layer(h, batch, j, norm_mix, norm_ffn, w_in_t, conv_w, a_log, dt_bias, o_gain, w_out,
               router, wg, wu, wd, *, seq_tile=1024):
    T, D = h.shape
    S = T // batch
    HV = a_log.shape[0]
    HK = HV // 2
    kw = HK * HEAD_DIM
    conv_ch = 4 * kw
    main = conv_ch + 2 * kw
    zeros = jnp.zeros((HV,), F32)
    aux = _aux_rows(jnp.concatenate([zeros, a_log]), jnp.concatenate([zeros, dt_bias]))
    hn, small = rmsnorm_small(h, norm_mix, _pad_cols(w_in_t[j, main:, :].T), aux,
                              functools.partial(_post_gdn, hv=HV), name="gdn_norm")
    small = chunk_cumsum(small, chunk=min(GDN_CHUNK, S), lo=HV, hi=2 * HV)
    C = GDN_CHUNK
    grow = small[:, HV:2 * HV].reshape(batch, S, HV).transpose(0, 2, 1).reshape(batch * HV, S // C, C)
    tn = min(1024, kw)
    tm = min(seq_tile, S)
    proj = matmul(hn, w_in_t, j, n_cols=main, tm=tm, tn=tn, out_dtype=BF16, w_transposed=True,
                  variants=functools.partial(_var_gdn_in, n_k_blocks=kw // tn,
                                             n_conv_blocks=conv_ch // tn, rows_per_seq=S // tm,
                                             q_scale=HEAD_DIM ** -0.5),
                  extra=(conv_w,),
                  extra_specs=(pl.BlockSpec((GDN_CONV, tn),
                                            lambda j, i: (0, jnp.minimum(j, conv_ch // tn - 1))),),
                  extra_scratch=(pltpu.VMEM((8 + tm, tn), F32),),
                  n_sub=8, name="gdn_in_proj")
    o = gated_delta(proj, small, grow, o_gain, batch=batch, k_heads=HK)
    h = matmul_residual(o, w_out, j, h, tm=min(1024, T), tn=min(512, D), name="gdn_out_proj")
    hn2, route = rmsnorm_small(h, norm_ffn, _pad_cols(router), _aux_rows(), _post_moe,
                               hn_dtype=F32, name="moe_norm")
    return moe_ffn(h, hn2, route, wg, wu, wd, j)


def kernel(x, norm_mix, norm_ffn, fox_w_in, fox_b_f, fox_q_norm, fox_k_norm, fox_w_out,
           gdn_w_in, gdn_conv, gdn_a_log, gdn_dt_bias, gdn_o_norm, gdn_w_out,
           ffn_w_gate, ffn_w_up, ffn_w_down,
           moe_router, moe_w_gate, moe_w_up, moe_w_down):
    B, S, D = x.shape
    h = x.reshape(B * S, D)
    depth = norm_mix.shape[0]
    fox_w_in_t = jnp.swapaxes(fox_w_in, 1, 2)
    gdn_w_in_t = jnp.swapaxes(gdn_w_in, 1, 2)
    for i in range(depth):
        j = i // 2
        if i % 2 == 0:
            h = _fox_layer(h, B, j, norm_mix[i], norm_ffn[i], fox_w_in_t, fox_b_f[j], fox_q_norm[j],
                           fox_k_norm[j], fox_w_out, ffn_w_gate, ffn_w_up, ffn_w_down)
        else:
            h = _gdn_layer(h, B, j, norm_mix[i], norm_ffn[i], gdn_w_in_t, gdn_conv[j], gdn_a_log[j],
                           gdn_dt_bias[j], gdn_o_norm[j], gdn_w_out, moe_router[j],
                           moe_w_gate, moe_w_up, moe_w_down)
    return h.reshape(B, S, D)
```

```python
import functools
import math

import jax
import jax.numpy as jnp
from jax import lax
from jax.experimental import pallas as pl
from jax.experimental.pallas import tpu as pltpu

F32 = jnp.float32
BF16 = jnp.bfloat16
EPS = 1e-6
LANES = 128
HEAD_DIM = 128
GDN_CHUNK = 64
GDN_CONV = 4
N_EXPERTS = 8
VMEM_LIMIT_BYTES = 56 * 1024 * 1024
HIGHEST = lax.Precision.HIGHEST
LOG2E = math.log2(math.e)
DMA_ISSUE_UNROLL = 8


def _params(*sem):
    return pltpu.CompilerParams(dimension_semantics=sem, vmem_limit_bytes=VMEM_LIMIT_BYTES)


def _log1pexp_neg_abs(x):
    return jnp.log(1.0 + jnp.exp(-jnp.abs(x)))


def _log_sigmoid(x):
    return jnp.minimum(x, 0.0) - _log1pexp_neg_abs(x)


def _softplus(x):
    return jnp.maximum(x, 0.0) + _log1pexp_neg_abs(x)


def _sigmoid(x):
    return 1.0 / (1.0 + jnp.exp(-x))


def _silu(x):
    return x * _sigmoid(x)


def _dot(a, b):
    return jnp.dot(a, b, preferred_element_type=F32)


def _dot_nt(a, b):
    return lax.dot_general(a, b, (((1,), (1,)), ((), ())), preferred_element_type=F32)


def _dot_tn(a, b):
    return lax.dot_general(a, b, (((0,), (0,)), ((), ())), preferred_element_type=F32)


def _rms(x, g):
    ms = jnp.mean(x * x, axis=-1, keepdims=True)
    return x * lax.rsqrt(ms + EPS) * g


def _norm_kernel(x_ref, g_ref, hn_ref):
    hn_ref[...] = _rms(x_ref[...], g_ref[...]).astype(hn_ref.dtype)


def _split_bf16(a):
    hi = a.astype(BF16)
    return hi, (a - hi.astype(F32)).astype(BF16)


def _norm_small_kernel(x_ref, g_ref, ws_hi_ref, ws_lo_ref, aux_ref, hn_ref, small_ref, *, post):
    y = _rms(x_ref[...], g_ref[...])
    hn_ref[...] = y.astype(hn_ref.dtype)
    y_hi, y_lo = _split_bf16(y)
    w_hi = ws_hi_ref[...]
    s = _dot(y_hi, w_hi) + (_dot(y_hi, ws_lo_ref[...]) + _dot(y_lo, w_hi))
    small_ref[...] = post(s, aux_ref[...])


def rmsnorm(x, g, *, tm=512):
    T, D = x.shape
    return pl.pallas_call(
        _norm_kernel,
        grid=(T // tm,),
        in_specs=[pl.BlockSpec((tm, D), lambda i: (i, 0)),
                  pl.BlockSpec((1, D), lambda i: (0, 0))],
        out_specs=pl.BlockSpec((tm, D), lambda i: (i, 0)),
        out_shape=jax.ShapeDtypeStruct((T, D), BF16),
        compiler_params=_params("parallel"),
        name="rmsnorm",
    )(x, g.reshape(1, D))


def rmsnorm_small(x, g, w_small, aux, post, *, hn_dtype=BF16, tm=512, name="rmsnorm_small"):
    T, D = x.shape
    w_hi, w_lo = _split_bf16(w_small)
    return pl.pallas_call(
        functools.partial(_norm_small_kernel, post=post),
        grid=(T // tm,),
        in_specs=[pl.BlockSpec((tm, D), lambda i: (i, 0)),
                  pl.BlockSpec((1, D), lambda i: (0, 0)),
                  pl.BlockSpec((D, LANES), lambda i: (0, 0)),
                  pl.BlockSpec((D, LANES), lambda i: (0, 0)),
                  pl.BlockSpec((8, LANES), lambda i: (0, 0))],
        out_specs=[pl.BlockSpec((tm, D), lambda i: (i, 0)),
                   pl.BlockSpec((tm, LANES), lambda i: (i, 0))],
        out_shape=[jax.ShapeDtypeStruct((T, D), hn_dtype),
                   jax.ShapeDtypeStruct((T, LANES), F32)],
        compiler_params=_params("parallel"),
        name=name,
    )(x, g.reshape(1, D), w_hi, w_lo, aux)


def _pad_cols(w, n=LANES):
    return jnp.pad(w, ((0, 0), (0, n - w.shape[1])))


def _aux_rows(*rows):
    out = [jnp.pad(r.astype(F32), (0, LANES - r.shape[0])) for r in rows]
    out += [jnp.zeros((LANES,), F32)] * (8 - len(out))
    return jnp.stack(out)


def _post_fox(s, aux):
    return _log_sigmoid(s + aux[0:1, :])


def _post_gdn(s, aux, *, hv):
    lane = lax.broadcasted_iota(jnp.int32, s.shape, 1)
    beta = _sigmoid(s)
    g = -jnp.exp(aux[0:1, :]) * _softplus(s + aux[1:2, :])
    return jnp.where(lane < hv, beta, g)


def _post_moe(s, aux):
    del aux
    lane = lax.broadcasted_iota(jnp.int32, s.shape, 1).astype(F32)
    neg = jnp.float32(-jnp.inf)
    logits = jnp.where(lane < N_EXPERTS, s, neg)
    v1 = jnp.max(logits, axis=-1, keepdims=True)
    i1 = jnp.min(jnp.where(logits == v1, lane, float(LANES)), axis=-1, keepdims=True)
    rest = jnp.where(lane == i1, neg, logits)
    v2 = jnp.max(rest, axis=-1, keepdims=True)
    i2 = jnp.min(jnp.where(rest == v2, lane, float(LANES)), axis=-1, keepdims=True)
    e2 = jnp.exp(v2 - v1)
    denom = 1.0 + e2
    out = jnp.where(lane == 0.0, i1, 0.0)
    out = jnp.where(lane == 1.0, i2, out)
    out = jnp.where(lane == 2.0, 1.0 / denom, out)
    out = jnp.where(lane == 3.0, e2 / denom, out)
    return out


def _cumsum_kernel(x_ref, o_ref, *, blk):
    S = x_ref.shape[0]
    r = lax.broadcasted_iota(jnp.int32, (blk, blk), 0)
    c = lax.broadcasted_iota(jnp.int32, (blk, blk), 1)
    tri = (r >= c).astype(F32)
    carry = jnp.zeros((1, LANES), F32)
    for i in range(S // blk):
        cs = jnp.dot(tri, x_ref[i * blk:(i + 1) * blk, :], preferred_element_type=F32,
                     precision=HIGHEST) + carry
        o_ref[i * blk:(i + 1) * blk, :] = cs
        carry = cs[blk - 1:blk, :]


def seq_cumsum(x, batch, *, blk=128):
    T = x.shape[0]
    S = T // batch
    blk = min(blk, S)
    return pl.pallas_call(
        functools.partial(_cumsum_kernel, blk=blk),
        grid=(batch,),
        in_specs=[pl.BlockSpec((S, LANES), lambda b: (b, 0))],
        out_specs=pl.BlockSpec((S, LANES), lambda b: (b, 0)),
        out_shape=jax.ShapeDtypeStruct((T, LANES), F32),
        compiler_params=_params("parallel"),
        name="seq_cumsum",
    )(x)


def _chunk_cumsum_kernel(x_ref, o_ref, *, chunk, lo, hi):
    R = x_ref.shape[0]
    r = lax.broadcasted_iota(jnp.int32, (chunk, chunk), 0)
    c = lax.broadcasted_iota(jnp.int32, (chunk, chunk), 1)
    tri = (r >= c).astype(F32)
    lane = lax.broadcasted_iota(jnp.int32, (chunk, LANES), 1)
    sel = (lane >= lo) & (lane < hi)
    for i in range(R // chunk):
        x = x_ref[i * chunk:(i + 1) * chunk, :]
        cs = jnp.dot(tri, x, preferred_element_type=F32, precision=HIGHEST)
        o_ref[i * chunk:(i + 1) * chunk, :] = jnp.where(sel, cs, x)


def chunk_cumsum(x, *, chunk, lo, hi, tm=512):
    T = x.shape[0]
    tm = min(tm, T)
    return pl.pallas_call(
        functools.partial(_chunk_cumsum_kernel, chunk=chunk, lo=lo, hi=hi),
        grid=(T // tm,),
        in_specs=[pl.BlockSpec((tm, LANES), lambda i: (i, 0))],
        out_specs=pl.BlockSpec((tm, LANES), lambda i: (i, 0)),
        out_shape=jax.ShapeDtypeStruct((T, LANES), F32),
        compiler_params=_params("parallel"),
        name="chunk_cumsum",
    )(x)


def _mm_kernel(a_ref, w_ref, *rest, n_extra, variants, n_sub, w_transposed):
    extra = rest[:n_extra]
    o_ref = rest[n_extra]
    scratch = rest[n_extra + 1:]
    wbf_ref = scratch[0]

    @pl.when(pl.program_id(1) == 0)
    def _():
        w = w_ref[...]
        wbf_ref[...] = (w.T if w_transposed else w).astype(BF16)

    rs = a_ref.shape[0] // n_sub
    for cond, fn in variants(pl.program_id(0), pl.program_id(1), extra, o_ref, scratch[1:]):
        @pl.when(cond)
        def _(fn=fn):
            pending = None
            for sub in range(n_sub):
                rows = slice(sub * rs, (sub + 1) * rs)
                acc = _dot(a_ref[rows, :], wbf_ref[...])
                if pending is not None:
                    fn(*pending)
                pending = (acc, rows)
            fn(*pending)


def _layer_w_spec(K, tn, layer):
    return pl.BlockSpec((None, K, tn), lambda j, i: (layer, 0, j))


def matmul(a, w, layer, *, n_cols, tm, tn, out_dtype, variants, extra=(), extra_specs=(),
           extra_scratch=(), n_sub=4, w_transposed=False, name="matmul"):
    M, K = a.shape
    grid = (n_cols // tn, M // tm)
    w_spec = (pl.BlockSpec((None, tn, K), lambda j, i: (layer, j, 0)) if w_transposed
              else _layer_w_spec(K, tn, layer))
    return pl.pallas_call(
        functools.partial(_mm_kernel, n_extra=len(extra), variants=variants, n_sub=n_sub,
                          w_transposed=w_transposed),
        grid=grid,
        in_specs=[pl.BlockSpec((tm, K), lambda j, i: (i, 0)),
                  w_spec,
                  *extra_specs],
        out_specs=pl.BlockSpec((tm, tn), lambda j, i: (i, j)),
        out_shape=jax.ShapeDtypeStruct((M, n_cols), out_dtype),
        scratch_shapes=[pltpu.VMEM((K, tn), BF16), *extra_scratch],
        compiler_params=_params("arbitrary", "arbitrary"),
        name=name,
    )(a, w, *extra)


def _var_residual(j, i, extra, o_ref, scratch):
    del j, i, scratch

    def fn(acc, rows):
        o_ref[rows, :] = extra[0][rows, :] + acc

    return [(True, fn)]


def matmul_residual(a, w, layer, res, *, tm, tn, name):
    N = w.shape[2]
    return matmul(a, w, layer, n_cols=N, tm=tm, tn=tn, out_dtype=F32, variants=_var_residual,
                  extra=(res,), extra_specs=(pl.BlockSpec((tm, tn), lambda j, i: (i, j)),),
                  name=name)


def _store_cast(o_ref):
    def fn(acc, rows):
        o_ref[rows, :] = acc.astype(o_ref.dtype)
    return fn


def _var_fox_in(j, i, extra, o_ref, scratch, *, n_q_blocks):
    del i, scratch
    qg_ref, kg_ref = extra
    tn = o_ref.shape[1]

    def normed(gain_ref, scale):
        def fn(acc, rows):
            for h in range(tn // HEAD_DIM):
                cols = slice(h * HEAD_DIM, (h + 1) * HEAD_DIM)
                o_ref[rows, cols] = (_rms(acc[:, cols], gain_ref[...]) * scale).astype(o_ref.dtype)
        return fn

    return [(j < n_q_blocks, normed(qg_ref, HEAD_DIM ** -0.5 * LOG2E)),
            ((j >= n_q_blocks) & (j < 2 * n_q_blocks), normed(kg_ref, 1.0)),
            (j >= 2 * n_q_blocks, _store_cast(o_ref))]


def _var_gdn_in(j, i, extra, o_ref, scratch, *, n_k_blocks, n_conv_blocks, rows_per_seq, q_scale):
    conv_ref, = extra
    buf_ref, = scratch
    tm, tn = o_ref.shape

    @pl.when(i % rows_per_seq == 0)
    def _():
        buf_ref[0:8, :] = jnp.zeros((8, tn), F32)

    def conv(l2norm):
        def fn(acc, rows):
            r0, r1 = rows.start, rows.stop
            buf_ref[8 + r0:8 + r1, :] = acc
            y = acc * conv_ref[GDN_CONV - 1:GDN_CONV, :]
            for tap in range(GDN_CONV - 1):
                shift = GDN_CONV - 1 - tap
                y = y + buf_ref[8 + r0 - shift:8 + r1 - shift, :] * conv_ref[tap:tap + 1, :]
            if r1 == tm:
                buf_ref[0:8, :] = acc[r1 - r0 - 8:, :]
            y = _silu(y)
            if l2norm:
                scale = jnp.where(j < n_k_blocks, jnp.float32(q_scale), jnp.float32(1.0))
                for h in range(tn // HEAD_DIM):
                    cols = slice(h * HEAD_DIM, (h + 1) * HEAD_DIM)
                    blk = y[:, cols]
                    ss = jnp.sum(blk * blk, axis=-1, keepdims=True)
                    o_ref[rows, cols] = (blk * lax.rsqrt(ss + EPS) * scale).astype(o_ref.dtype)
            else:
                o_ref[rows, :] = y.astype(o_ref.dtype)
        return fn

    return [(j < 2 * n_k_blocks, conv(True)),
            ((j >= 2 * n_k_blocks) & (j < n_conv_blocks), conv(False)),
            (j >= n_conv_blocks, _store_cast(o_ref))]


def _swiglu_up_kernel(a_ref, wg_ref, wu_ref, o_ref, wg_bf, wu_bf):
    @pl.when(pl.program_id(1) == 0)
    def _():
        wg_bf[...] = wg_ref[...].astype(BF16)
        wu_bf[...] = wu_ref[...].astype(BF16)

    a = a_ref[...]
    g = _dot(a, wg_bf[...])
    u = _dot(a, wu_bf[...])
    o_ref[...] = (_silu(g) * u).astype(o_ref.dtype)


def swiglu_up(a, wg, wu, layer, *, tm, tn):
    M, K = a.shape
    N = wg.shape[2]
    return pl.pallas_call(
        _swiglu_up_kernel,
        grid=(N // tn, M // tm),
        in_specs=[pl.BlockSpec((tm, K), lambda j, i: (i, 0)),
                  _layer_w_spec(K, tn, layer),
                  _layer_w_spec(K, tn, layer)],
        out_specs=pl.BlockSpec((tm, tn), lambda j, i: (i, j)),
        out_shape=jax.ShapeDtypeStruct((M, N), BF16),
        scratch_shapes=[pltpu.VMEM((K, tn), BF16), pltpu.VMEM((K, tn), BF16)],
        compiler_params=_params("arbitrary", "arbitrary"),
        name="swiglu_up",
    )(a, wg, wu)


def _fox_attn_kernel(q_ref, k_ref, v_ref, og_ref, cq_ref, ck_ref, o_ref, *scratch, tq, nh):
    tk = tq
    hp = pl.program_id(1)
    qi = pl.program_id(2)
    heads = range(nh)
    s_scr, p_scr, m_ref, l_ref, alpha_ref, acc_ref = (scratch[i * nh:(i + 1) * nh] for i in range(6))
    cols = lambda e: slice(e * HEAD_DIM, (e + 1) * HEAD_DIM)
    lane = lax.broadcasted_iota(jnp.int32, cq_ref.shape, 1)
    cq_all = cq_ref[...]
    cq = [jnp.sum(jnp.where(lane == nh * hp + e, cq_all, 0.0), axis=-1, keepdims=True) * LOG2E for e in heads]

    def ck(e, kb):
        return ck_ref[e, pl.ds(kb, 1), :] * LOG2E

    def scores(e, kb):
        return _dot_nt(q_ref[:, cols(e)], k_ref[pl.ds(pl.multiple_of(kb * tk, tk), tk), cols(e)])

    def pv_dot(e, kb, slot):
        return _dot(p_scr[e][slot], v_ref[pl.ds(pl.multiple_of(kb * tk, tk), tk), cols(e)])

    def fold(e, pv):
        acc_ref[e][...] = alpha_ref[e][...] * acc_ref[e][...] + pv

    def softmax(e, t, slot, between):
        m_prev = m_ref[e][...]
        m_new = jnp.maximum(m_prev, jnp.max(t, axis=-1, keepdims=True) + cq[e])
        between()
        alpha = jnp.exp2(m_prev - m_new)
        p = jnp.exp2(t + (cq[e] - m_new))
        l_ref[e][...] = alpha * l_ref[e][...] + jnp.sum(p, axis=-1, keepdims=True)
        alpha_ref[e][...] = alpha
        p_scr[e][slot] = p.astype(p_scr[e].dtype)
        m_ref[e][...] = m_new

    for e in heads:
        m_ref[e][...] = jnp.full(m_ref[e].shape, -jnp.inf, F32)
        l_ref[e][...] = jnp.zeros(l_ref[e].shape, F32)
        alpha_ref[e][...] = jnp.zeros(alpha_ref[e].shape, F32)
        acc_ref[e][...] = jnp.zeros(acc_ref[e].shape, F32)
        p_scr[e][1] = jnp.zeros(p_scr[e].shape[1:], p_scr[e].dtype)
        s_scr[e][0] = scores(e, 0)

    def body(k, carry):
        cur = k % 2
        pv = [pv_dot(e, jnp.maximum(k - 1, 0), 1 - cur) for e in heads]
        s_next = [scores(e, k + 1) for e in heads]
        for e in heads:
            t = s_scr[e][cur] - ck(e, k)

            def between(e=e):
                fold(e, pv[e])
                s_scr[e][1 - cur] = s_next[e]

            softmax(e, t, cur, between)
        return carry

    lax.fori_loop(0, qi, body, 0)

    cur = qi % 2
    pv = [pv_dot(e, jnp.maximum(qi - 1, 0), 1 - cur) for e in heads]
    causal = (lax.broadcasted_iota(jnp.int32, (tq, tk), 0) >= lax.broadcasted_iota(jnp.int32, (tq, tk), 1))
    last_pv = []
    for e in heads:
        t = jnp.where(causal, s_scr[e][cur] - ck(e, qi), -jnp.inf)
        softmax(e, t, cur, lambda e=e: fold(e, pv[e]))
        last_pv.append(pv_dot(e, qi, cur))
    for e in heads:
        fold(e, last_pv[e])
        o = acc_ref[e][...] / l_ref[e][...]
        o_ref[:, cols(e)] = (o * _sigmoid(og_ref[:, cols(e)].astype(F32))).astype(o_ref.dtype)


def fox_attention(proj, cum, cum_rows, *, batch, heads, tq=512, nh=4):
    T = proj.shape[0]
    S = T // batch
    nq = S // tq
    H = heads
    nh = min(nh, H)
    G = H // nh
    w = nh * HEAD_DIM
    kern = functools.partial(_fox_attn_kernel, tq=tq, nh=nh)
    return pl.pallas_call(
        kern,
        grid=(batch, G, nq),
        in_specs=[
            pl.BlockSpec((tq, w), lambda b, g, qi: (b * nq + qi, g)),
            pl.BlockSpec((S, w), lambda b, g, qi: (b, G + g)),
            pl.BlockSpec((S, w), lambda b, g, qi: (b, 2 * G + g)),
            pl.BlockSpec((tq, w), lambda b, g, qi: (b * nq + qi, 3 * G + g)),
            pl.BlockSpec((tq, LANES), lambda b, g, qi: (b * nq + qi, 0)),
            pl.BlockSpec((nh, nq, tq), lambda b, g, qi: (b * G + g, 0, 0)),
        ],
        out_specs=pl.BlockSpec((tq, w), lambda b, g, qi: (b * nq + qi, g)),
        out_shape=jax.ShapeDtypeStruct((T, H * HEAD_DIM), BF16),
        scratch_shapes=[*[pltpu.VMEM((2, tq, tq), F32)] * nh, *[pltpu.VMEM((2, tq, tq), BF16)] * nh,
                        *[pltpu.VMEM((tq, 1), F32)] * (3 * nh), *[pltpu.VMEM((tq, HEAD_DIM), F32)] * nh],
        compiler_params=_params("parallel", "parallel", "arbitrary"),
        name="fox_attention",
    )(proj, proj, proj, proj, cum, cum_rows)


def _gdn_kernel(q_ref, k_ref, v_ref, z_ref, small_ref, grow_ref, gain_ref,
                o_ref, cols_ref, u_ref, w_ref, a_ref, o_scr, *, hv, nk, group):
    C = GDN_CHUNK
    S = q_ref.shape[0]
    n_chunks = S // C
    nv = 2 * nk
    hp = pl.program_id(1)
    cols = lambda i: slice(i * HEAD_DIM, (i + 1) * HEAD_DIM)

    cols_ref[...] = pltpu.roll(small_ref[...], (LANES - nv * hp) % LANES, 1)

    ri = lax.broadcasted_iota(jnp.int32, (C, C), 0)
    ci = lax.broadcasted_iota(jnp.int32, (C, C), 1)
    incl = ri >= ci
    strict = ri > ci
    eye = (ri == ci).astype(F32)
    steps = int(math.log2(C)) - 1

    def prep_stages(gi):
        chunk_rows = [pl.ds(pl.multiple_of((gi * group + t) * C, C), C) for t in range(group)]
        kf, qkk = {}, {}
        for t, rows in enumerate(chunk_rows):
            for kh in range(nk):
                kc = k_ref[rows, cols(kh)]
                qc = q_ref[rows, cols(kh)]
                kf[t, kh] = kc.astype(F32)
                qkk[t, kh] = _dot_nt(jnp.concatenate([kc, qc], axis=0), kc)
        yield
        chains = [(t, h) for t in range(group) for h in range(nv)]
        beta, g, x, lb = {}, {}, {}, {}
        for t, h in chains:
            rows = chunk_rows[t]
            g[t, h] = cols_ref[rows, hv + h:hv + h + 1]
            beta[t, h] = cols_ref[rows, h:h + 1]
            grow = grow_ref[h, pl.ds(gi * group + t, 1), :]
            decay = jnp.exp(jnp.where(incl, g[t, h] - grow, -jnp.inf))
            lmat = jnp.where(strict, beta[t, h] * qkk[t, h // 2][:C] * decay, 0.0)
            a_ref[h, rows, :] = jnp.where(incl, qkk[t, h // 2][C:] * decay, 0.0)
            x[t, h] = eye - lmat
            lb[t, h] = lmat.astype(BF16)
        p = {ch: _dot(lb[ch], lb[ch]) for ch in chains}
        yield
        for s in range(steps):
            if s < steps - 1:
                px = {ch: _dot(jnp.concatenate([p[ch], x[ch]], axis=0).astype(BF16), p[ch].astype(BF16))
                      for ch in chains}
                for ch in chains:
                    p[ch], x[ch] = px[ch][:C], x[ch] + px[ch][C:]
            else:
                xp = {ch: _dot(x[ch].astype(BF16), p[ch].astype(BF16)) for ch in chains}
                for ch in chains:
                    x[ch] = x[ch] + xp[ch]
            yield
        sol = {}
        for t, h in chains:
            vf = v_ref[chunk_rows[t], cols(h)].astype(F32)
            rhs = jnp.concatenate([vf * beta[t, h], kf[t, h // 2] * (beta[t, h] * jnp.exp(g[t, h]))], axis=1)
            sol[t, h] = _dot(x[t, h].astype(BF16), rhs.astype(BF16))
        for t, h in chains:
            u_ref[h, chunk_rows[t], :] = sol[t, h][:, :HEAD_DIM]
            w_ref[h, chunk_rows[t], :] = sol[t, h][:, HEAD_DIM:]
        yield

    def rec_stages(gi, states):
        for t in range(group):
            c = gi * group + t
            rows = pl.ds(pl.multiple_of(c * C, C), C)
            last = pl.ds(c * C + C - 1, 1)
            kf = [k_ref[rows, cols(kh)].astype(F32) for kh in range(nk)]
            qf = [q_ref[rows, cols(kh)].astype(F32) for kh in range(nk)]
            g = [cols_ref[rows, hv + h:hv + h + 1] for h in range(nv)]
            g_last = [cols_ref[last, hv + h:hv + h + 1] for h in range(nv)]
            ws_qs = []
            for h in range(nv):
                wq = jnp.concatenate([w_ref[h, rows, :], qf[h // 2] * jnp.exp(g[h])], axis=0)
                ws_qs.append(_dot(wq.astype(BF16), states[h].astype(BF16)))
            yield
            vb = [(u_ref[h, rows, :] - ws_qs[h][:C]).astype(BF16) for h in range(nv)]
            av = [_dot(a_ref[h, rows, :].astype(BF16), vb[h]) for h in range(nv)]
            kv = [_dot_tn((kf[h // 2] * jnp.exp(g_last[h] - g[h])).astype(BF16), vb[h]) for h in range(nv)]
            for h in range(nv):
                o_scr[h, rows, :] = ws_qs[h][C:] + av[h]
                states[h] = states[h] * jnp.exp(g_last[h]) + kv[h]
            yield

    def run_interleaved(*gens):
        live = list(gens)
        while live:
            for gen in list(live):
                if next(gen, "done") == "done":
                    live.remove(gen)

    n_groups = n_chunks // group
    run_interleaved(prep_stages(0))

    def body(gi, states):
        states = list(states)
        run_interleaved(rec_stages(gi, states), prep_stages(gi + 1))
        return tuple(states)

    zero = jnp.zeros((HEAD_DIM, HEAD_DIM), F32)
    states = list(lax.fori_loop(0, n_groups - 1, body, (zero,) * nv))
    run_interleaved(rec_stages(n_groups - 1, states))

    for h in range(nv):
        o = _rms(o_scr[h], gain_ref[...])
        z = z_ref[:, cols(h)].astype(F32)
        o_ref[:, cols(h)] = (o * _silu(z)).astype(o_ref.dtype)


def gated_delta(proj, small, grow, o_gain, *, batch, k_heads, nk=2, group=4):
    T = proj.shape[0]
    S = T // batch
    HK = k_heads
    HV = 2 * HK
    C = GDN_CHUNK
    n_chunks = S // C
    nk = min(nk, HK)
    nv = 2 * nk
    NP = HK // nk
    group = min(group, n_chunks)
    kern = functools.partial(_gdn_kernel, hv=HV, nk=nk, group=group)
    kw = nk * HEAD_DIM
    vw = nv * HEAD_DIM
    return pl.pallas_call(
        kern,
        grid=(batch, NP),
        in_specs=[
            pl.BlockSpec((S, kw), lambda b, h: (b, h)),
            pl.BlockSpec((S, kw), lambda b, h: (b, NP + h)),
            pl.BlockSpec((S, vw), lambda b, h: (b, NP + h)),
            pl.BlockSpec((S, vw), lambda b, h: (b, 2 * NP + h)),
            pl.BlockSpec((S, LANES), lambda b, h: (b, 0)),
            pl.BlockSpec((nv, n_chunks, C), lambda b, h: (b * NP + h, 0, 0)),
            pl.BlockSpec((1, HEAD_DIM), lambda b, h: (0, 0)),
        ],
        out_specs=pl.BlockSpec((S, vw), lambda b, h: (b, h)),
        out_shape=jax.ShapeDtypeStruct((T, HV * HEAD_DIM), BF16),
        scratch_shapes=[pltpu.VMEM((S, LANES), F32),
                        pltpu.VMEM((nv, S, HEAD_DIM), F32), pltpu.VMEM((nv, S, HEAD_DIM), F32),
                        pltpu.VMEM((nv, S, C), F32), pltpu.VMEM((nv, S, HEAD_DIM), F32)],
        compiler_params=_params("parallel", "parallel"),
        name="gated_delta",
    )(proj, proj, proj, proj, small, grow, o_gain.reshape(1, HEAD_DIM))


def _row_copy(src_ref, dst_ref, src_row, dst_row, sem):
    return pltpu.make_async_copy(src_ref.at[pl.ds(src_row, 1), :], dst_ref.at[pl.ds(dst_row, 1), :], sem)


def _gather_rows_kernel(idx_ref, nrows_ref, src_ref, o_ref, buf_ref, sem, *, rows):
    t = pl.program_id(0)
    slot = t % 2
    live = lambda step: step * rows < nrows_ref[0]

    def issue(step, sl):
        base = step * rows

        def start(g, c):
            for u in range(DMA_ISSUE_UNROLL):
                r = g * DMA_ISSUE_UNROLL + u
                _row_copy(src_ref, buf_ref.at[sl], idx_ref[base + r], r, sem.at[sl]).start(priority=u % 2)
            return c

        lax.fori_loop(0, rows // DMA_ISSUE_UNROLL, start, 0)

    @pl.when(t == 0)
    def _():
        issue(0, 0)

    @pl.when((t + 1 < pl.num_programs(0)) & live(t + 1))
    def _():
        issue(t + 1, 1 - slot)

    @pl.when(live(t))
    def _():
        def wait(r, c):
            _row_copy(src_ref, buf_ref.at[slot], 0, r, sem.at[slot]).wait()
            return c

        lax.fori_loop(0, rows, wait, 0, unroll=8)
        o_ref[...] = buf_ref[slot].astype(o_ref.dtype)

    @pl.when(jnp.logical_not(live(t)))
    def _():
        o_ref[...] = jnp.zeros(o_ref.shape, o_ref.dtype)


def gather_rows(src, idx, n_rows, *, rows=256):
    P = idx.shape[0]
    D = src.shape[1]
    return pl.pallas_call(
        functools.partial(_gather_rows_kernel, rows=rows),
        grid_spec=pltpu.PrefetchScalarGridSpec(
            num_scalar_prefetch=2,
            grid=(P // rows,),
            in_specs=[pl.BlockSpec(memory_space=pl.ANY)],
            out_specs=pl.BlockSpec((rows, D), lambda t, idx, nr: (t, 0)),
            scratch_shapes=[pltpu.VMEM((2, rows, D), F32), pltpu.SemaphoreType.DMA((2,))]),
        out_shape=jax.ShapeDtypeStruct((P, D), BF16),
        compiler_params=_params("arbitrary"),
        name="moe_gather",
    )(idx, n_rows, src)


def _combine_kernel(pos_ref, y_ref, h_ref, small_ref, o_ref, buf_ref, sem, *, rows):
    t = pl.program_id(0)
    slot = t % 2

    def issue(step, sl):
        base = step * rows

        def start(g, c):
            for u in range(DMA_ISSUE_UNROLL // 2):
                r = g * (DMA_ISSUE_UNROLL // 2) + u
                for k in range(2):
                    _row_copy(y_ref, buf_ref.at[sl, k], pos_ref[k, base + r], r, sem.at[sl]).start(priority=k)
            return c

        lax.fori_loop(0, rows // (DMA_ISSUE_UNROLL // 2), start, 0)

    @pl.when(t == 0)
    def _():
        issue(0, 0)

    @pl.when(t + 1 < pl.num_programs(0))
    def _():
        issue(t + 1, 1 - slot)

    def wait(r, c):
        for k in range(2):
            _row_copy(y_ref, buf_ref.at[slot, k], 0, r, sem.at[slot]).wait()
        return c

    lax.fori_loop(0, rows, wait, 0, unroll=4)
    w1 = small_ref[:, 2:3]
    w2 = small_ref[:, 3:4]
    o_ref[...] = h_ref[...] + (w1 * buf_ref[slot, 0] + w2 * buf_ref[slot, 1])


def moe_combine(y, pos, h, small, *, rows=256):
    T, D = h.shape
    return pl.pallas_call(
        functools.partial(_combine_kernel, rows=rows),
        grid_spec=pltpu.PrefetchScalarGridSpec(
            num_scalar_prefetch=1,
            grid=(T // rows,),
            in_specs=[pl.BlockSpec(memory_space=pl.ANY),
                      pl.BlockSpec((rows, D), lambda t, pos: (t, 0)),
                      pl.BlockSpec((rows, LANES), lambda t, pos: (t, 0))],
            out_specs=pl.BlockSpec((rows, D), lambda t, pos: (t, 0)),
            scratch_shapes=[pltpu.VMEM((2, 2, rows, D), F32), pltpu.SemaphoreType.DMA((2,))]),
        out_shape=jax.ShapeDtypeStruct((T, D), F32),
        compiler_params=_params("arbitrary"),
        name="moe_combine",
    )(pos, y, h, small)


def _grouped_kernel(te_ref, nv_ref, nxt_ref, x_ref, *rest, n_w, layer, tn, finish):
    w_hbm = rest[:n_w]
    o_ref = rest[n_w]
    stage = rest[n_w + 1:2 * n_w + 1]
    wbf = rest[2 * n_w + 1:3 * n_w + 1]
    sem = rest[3 * n_w + 1]
    j = pl.program_id(0)
    t = pl.program_id(1)
    e = te_ref[t]
    first = (t == 0) | (e != te_ref[jnp.maximum(t - 1, 0)])
    valid = t < nv_ref[0]

    def copies(jj, ee):
        cols = pl.ds(pl.multiple_of(jj * tn, LANES), tn)
        return [pltpu.make_async_copy(w_hbm[i].at[layer, ee, :, cols], stage[i], sem.at[i])
                for i in range(n_w)]

    @pl.when((j == 0) & (t == 0))
    def _():
        for c in copies(j, e):
            c.start()

    @pl.when(valid & first)
    def _():
        for i, c in enumerate(copies(j, e)):
            c.wait()
            wbf[i][...] = stage[i][...].astype(BF16)
        nt = nxt_ref[t]

        @pl.when(nt >= 0)
        def _():
            for c in copies(j, te_ref[jnp.maximum(nt, 0)]):
                c.start()

        @pl.when((nt < 0) & (j + 1 < pl.num_programs(0)))
        def _():
            for c in copies(j + 1, te_ref[0]):
                c.start()

    @pl.when(valid)
    def _():
        x = x_ref[...]
        o_ref[...] = finish([_dot(x, w[...]) for w in wbf]).astype(o_ref.dtype)

    @pl.when(jnp.logical_not(valid))
    def _():
        o_ref[...] = jnp.zeros(o_ref.shape, o_ref.dtype)


def grouped_matmul(x, weights, layer, tile_expert, n_valid, next_tile, *, tm, tn, out_dtype, finish, name):
    P, K = x.shape
    N = weights[0].shape[3]
    n_w = len(weights)
    last = lambda t, nv: jnp.minimum(t, nv[0] - 1)
    return pl.pallas_call(
        functools.partial(_grouped_kernel, n_w=n_w, layer=layer, tn=tn, finish=finish),
        grid_spec=pltpu.PrefetchScalarGridSpec(
            num_scalar_prefetch=3,
            grid=(N // tn, P // tm),
            in_specs=[pl.BlockSpec((tm, K), lambda j, t, te, nv, nx: (last(t, nv), 0)),
                      *[pl.BlockSpec(memory_space=pl.ANY)] * n_w],
            out_specs=pl.BlockSpec((tm, tn), lambda j, t, te, nv, nx: (t, j)),
            scratch_shapes=[*[pltpu.VMEM((K, tn), F32)] * n_w, *[pltpu.VMEM((K, tn), BF16)] * n_w,
                            pltpu.SemaphoreType.DMA((n_w,))]),
        out_shape=jax.ShapeDtypeStruct((P, N), out_dtype),
        compiler_params=_params("arbitrary", "arbitrary"),
        name=name,
    )(tile_expert, n_valid, next_tile, x, *weights)


def _moe_plan(small, *, tm):
    T = small.shape[0]
    experts = small[:, 0:2].astype(jnp.int32)
    flat = experts.T.reshape(-1)
    onehot = (flat[:, None] == jnp.arange(N_EXPERTS)[None, :]).astype(jnp.int32)
    rank = jnp.take_along_axis(jnp.cumsum(onehot, axis=0) - onehot, flat[:, None], axis=1)[:, 0]
    counts = jnp.sum(onehot, axis=0)
    tiles_per = (counts + tm - 1) // tm
    tile_end = jnp.cumsum(tiles_per)
    tile_start = tile_end - tiles_per
    pos = tile_start[flat] * tm + rank
    n_tiles = (2 * T) // tm + N_EXPERTS
    n_valid = tile_end[-1]
    tile_ids = jnp.minimum(jnp.arange(n_tiles), n_valid - 1)
    tile_expert = jnp.sum((tile_ids[:, None] >= tile_end[None, :]).astype(jnp.int32), axis=1)
    token = jnp.tile(jnp.arange(T, dtype=jnp.int32), 2)
    row_token = jnp.zeros((n_tiles * tm,), jnp.int32).at[pos].set(token)
    next_tile = jnp.where(tile_end[tile_expert] < n_valid, tile_end[tile_expert], -1)
    return (pos.reshape(2, T).astype(jnp.int32), row_token, tile_expert.astype(jnp.int32),
            n_valid.reshape(1).astype(jnp.int32), next_tile.astype(jnp.int32))


def moe_ffn(h, hn, small, wg, wu, wd, layer, *, tm=512):
    tm = min(tm, h.shape[0])
    pos, row_token, tile_expert, n_valid, next_tile = _moe_plan(small, tm=tm)
    xs = gather_rows(hn, row_token, n_valid * tm, rows=min(256, tm))
    N = wg.shape[3]
    tn_up = N // 2 if (N // 2) % LANES == 0 else N
    mid = grouped_matmul(xs, (wg, wu), layer, tile_expert, n_valid, next_tile, tm=tm, tn=tn_up,
                         out_dtype=BF16, finish=lambda d: _silu(d[0]) * d[1], name="expert_up")
    y = grouped_matmul(mid, (wd,), layer, tile_expert, n_valid, next_tile, tm=tm,
                       tn=min(1024, wd.shape[3]), out_dtype=F32, finish=lambda d: d[0], name="expert_down")
    return moe_combine(y, pos, h, small, rows=min(256, tm))


def _fox_layer(h, batch, j, norm_mix, norm_ffn, w_in_t, b_f, q_gain, k_gain, w_out, w_gate, w_up,
               w_down):
    T, D = h.shape
    S = T // batch
    H = b_f.shape[0]
    width = H * HEAD_DIM
    hn, lf = rmsnorm_small(h, norm_mix, _pad_cols(w_in_t[j, 4 * width:, :].T), _aux_rows(b_f), _post_fox,
                           name="fox_norm")
    cum = seq_cumsum(lf, batch)
    tq = min(512, S)
    cum_rows = cum[:, :H].reshape(batch, S, H).transpose(0, 2, 1).reshape(batch * H, S // tq, tq)
    tn = min(1024, width)
    proj = matmul(hn, w_in_t, j, n_cols=4 * width, tm=min(1024, T), tn=tn, out_dtype=BF16, w_transposed=True,
                  variants=functools.partial(_var_fox_in, n_q_blocks=width // tn),
                  extra=(q_gain.reshape(1, HEAD_DIM), k_gain.reshape(1, HEAD_DIM)),
                  extra_specs=(pl.BlockSpec((1, HEAD_DIM), lambda j, i: (0, 0)),) * 2,
                  name="fox_in_proj")
    o = fox_attention(proj, cum, cum_rows, batch=batch, heads=H, tq=tq)
    h = matmul_residual(o, w_out, j, h, tm=min(1024, T), tn=min(1024, D), name="fox_out_proj")
    hn2 = rmsnorm(h, norm_ffn)
    F = w_gate.shape[2]
    mid = swiglu_up(hn2, w_gate, w_up, j, tm=min(1024, T), tn=512 if F % 512 == 0 else F)
    return matmul_residual(mid, w_down, j, h, tm=min(512, T), tn=min(512, D), name="ffn_down")


def _gdn_layer(h, batch, j, norm_mix, norm_ffn, w_in_t, conv_w, a_log, dt_bias, o_gain, w_out,
               router, wg, wu, wd, *, seq_tile=1024):
    T, D = h.shape
    S = T // batch
    HV = a_log.shape[0]
    HK = HV // 2
    kw = HK * HEAD_DIM
    conv_ch = 4 * kw
    main = conv_ch + 2 * kw
    zeros = jnp.zeros((HV,), F32)
    aux = _aux_rows(jnp.concatenate([zeros, a_log]), jnp.concatenate([zeros, dt_bias]))
    hn, small = rmsnorm_small(h, norm_mix, _pad_cols(w_in_t[j, main:, :].T), aux,
                              functools.partial(_post_gdn, hv=HV), name="gdn_norm")
    small = chunk_cumsum(small, chunk=min(GDN_CHUNK, S), lo=HV, hi=2 * HV)
    C = GDN_CHUNK
    grow = small[:, HV:2 * HV].reshape(batch, S, HV).transpose(0, 2, 1).reshape(batch * HV, S // C, C)
    tn = min(1024, kw)
    tm = min(seq_tile, S)
    proj = matmul(hn, w_in_t, j, n_cols=main, tm=tm, tn=tn, out_dtype=BF16, w_transposed=True,
                  variants=functools.partial(_var_gdn_in, n_k_blocks=kw // tn,
                                             n_conv_blocks=conv_ch // tn, rows_per_seq=S // tm,
                                             q_scale=HEAD_DIM ** -0.5),
                  extra=(conv_w,),
                  extra_specs=(pl.BlockSpec((GDN_CONV, tn),
                                            lambda j, i: (0, jnp.minimum(j, conv_ch // tn - 1))),),
                  extra_scratch=(pltpu.VMEM((8 + tm, tn), F32),),
                  name="gdn_in_proj")
    o = gated_delta(proj, small, grow, o_gain, batch=batch, k_heads=HK)
    h = matmul_residual(o, w_out, j, h, tm=min(1024, T), tn=min(512, D), name="gdn_out_proj")
    hn2, route = rmsnorm_small(h, norm_ffn, _pad_cols(router), _aux_rows(), _post_moe,
                               hn_dtype=F32, name="moe_norm")
    return moe_ffn(h, hn2, route, wg, wu, wd, j)


def kernel(x, norm_mix, norm_ffn, fox_w_in, fox_b_f, fox_q_norm, fox_k_norm, fox_w_out,
           gdn_w_in, gdn_conv, gdn_a_log, gdn_dt_bias, gdn_o_norm, gdn_w_out,
           ffn_w_gate, ffn_w_up, ffn_w_down,
           moe_router, moe_w_gate, moe_w_up, moe_w_down):
    B, S, D = x.shape
    h = x.reshape(B * S, D)
    depth = norm_mix.shape[0]
    fox_w_in_t = jnp.swapaxes(fox_w_in, 1, 2)
    gdn_w_in_t = jnp.swapaxes(gdn_w_in, 1, 2)
    for i in range(depth):
        j = i // 2
        if i % 2 == 0:
            h = _fox_layer(h, B, j, norm_mix[i], norm_ffn[i], fox_w_in_t, fox_b_f[j], fox_q_norm[j],
                           fox_k_norm[j], fox_w_out, ffn_w_gate, ffn_w_up, ffn_w_down)
        else:
            h = _gdn_layer(h, B, j, norm_mix[i], norm_ffn[i], gdn_w_in_t, gdn_conv[j], gdn_a_log[j],
                           gdn_dt_bias[j], gdn_o_norm[j], gdn_w_out, moe_router[j],
                           moe_w_gate, moe_w_up, moe_w_down)
    return h.reshape(B, S, D)
```
